```python
import math
import jax, jax.numpy as jnp
from jax import lax
import numpy as np


D_MODEL = 4096
BATCH = 2
SEQ = 4096
DEPTH = 2

HEAD_DIM = 128
N_HEADS = D_MODEL // HEAD_DIM
N_KV_HEADS = max(1, N_HEADS // 4)
GROUP = N_HEADS // N_KV_HEADS
Q_DIM = N_HEADS * HEAD_DIM
KV_DIM = N_KV_HEADS * HEAD_DIM
QKV_DIM = Q_DIM + 2 * KV_DIM
D_FF = ((8 * D_MODEL + 3 * 256 - 1) // (3 * 256)) * 256
ROPE_THETA = 10000.0
WINDOW_A_HALF = 128
DILATED_BRANCHES = ((128, 1), (512, 4), (2048, 16))
N_MIXERS = 2
N_A_LAYERS = (DEPTH + N_MIXERS - 1) // N_MIXERS
RMS_EPS = 1e-6
NEG_INF = -1e30

kernel_name = "hybrid_window_sink_dilated_gqa_swiglu_adaln"


def rms_norm(x, g):
    xf = x.astype(jnp.float32)
    y = xf * lax.rsqrt(jnp.mean(xf * xf, axis=-1, keepdims=True) + RMS_EPS)
    return (y * g.astype(jnp.float32)).astype(x.dtype)


def rope(t, positions):
    half = HEAD_DIM // 2
    inv_freq = jnp.exp(-math.log(ROPE_THETA) * jnp.arange(half, dtype=jnp.float32) / half)
    ang = positions.astype(jnp.float32)[:, None, :, None] * inv_freq
    cos, sin = jnp.cos(ang), jnp.sin(ang)
    tf = t.astype(jnp.float32)
    t1, t2 = tf[..., :half], tf[..., half:]
    return jnp.concatenate([t1 * cos - t2 * sin, t2 * cos + t1 * sin], axis=-1).astype(t.dtype)


def banded_attention(q, k, v, half_w, sink=None):
    blk = half_w
    n, hkv, g, L, hd = q.shape
    nb = -(-L // blk)
    Lp = nb * blk
    qb = jnp.pad(q, ((0, 0), (0, 0), (0, 0), (0, Lp - L), (0, 0))).reshape(n, hkv, g, nb, blk, hd)
    pad_kv = ((0, 0), (0, 0), (blk, Lp - L + blk), (0, 0))
    kp, vp = jnp.pad(k, pad_kv), jnp.pad(v, pad_kv)

    def band(t):
        return jnp.concatenate(
            [t[:, :, j * blk: j * blk + Lp].reshape(n, hkv, nb, blk, hd) for j in range(3)], axis=3)

    kb, vb = band(kp), band(vp)
    qpos = jnp.arange(nb)[:, None] * blk + jnp.arange(blk)[None, :]
    kpos = (jnp.arange(nb)[:, None] - 1) * blk + jnp.arange(3 * blk)[None, :]
    valid = ((jnp.abs(qpos[:, :, None] - kpos[:, None, :]) <= half_w)
             & (kpos[:, None, :] >= 0) & (kpos[:, None, :] < L))
    s = jnp.einsum('nhgbqd,nhbkd->nhgbqk', qb, kb,
                   preferred_element_type=jnp.float32) * (1.0 / math.sqrt(hd))
    s = jnp.where(valid, s, NEG_INF)
    m = jnp.max(s, axis=-1)
    if sink is not None:
        sink_b = sink.astype(jnp.float32)[None, :, :, None, None]
        m = jnp.maximum(m, sink_b)
    p = jnp.exp(s - m[..., None])
    denom = jnp.sum(p, axis=-1)
    if sink is not None:
        denom = denom + jnp.exp(sink_b - m)
    o = jnp.einsum('nhgbqk,nhbkd->nhgbqd', p, vb.astype(jnp.float32)) / denom[..., None]
    o = o.reshape(n, hkv, g, Lp, hd)[:, :, :, :L]
    lse = (m + jnp.log(denom)).reshape(n, hkv, g, Lp)[:, :, :, :L]
    return o.astype(q.dtype), lse


def project_qkv(h, w_qkv, positions):
    b, s, _ = h.shape
    qkv = h @ w_qkv
    q = qkv[..., :Q_DIM].reshape(b, s, N_HEADS, HEAD_DIM).transpose(0, 2, 1, 3)
    k = qkv[..., Q_DIM:Q_DIM + KV_DIM].reshape(b, s, N_KV_HEADS, HEAD_DIM).transpose(0, 2, 1, 3)
    v = qkv[..., Q_DIM + KV_DIM:].reshape(b, s, N_KV_HEADS, HEAD_DIM).transpose(0, 2, 1, 3)
    q = rope(q, positions).reshape(b, N_KV_HEADS, GROUP, s, HEAD_DIM)
    k = rope(k, positions)
    return q, k, v


def merge_heads(o, w_o):
    b, _, _, s, _ = o.shape
    return o.transpose(0, 3, 1, 2, 4).reshape(b, s, Q_DIM) @ w_o


def mixer_window_sink(h, w_qkv, w_o, sink, positions):
    q, k, v = project_qkv(h, w_qkv, positions)
    o, _ = banded_attention(q, k, v, WINDOW_A_HALF, sink.reshape(N_KV_HEADS, GROUP))
    return merge_heads(o, w_o)


def mixer_dilated(h, w_qkv, w_o, positions):
    q, k, v = project_qkv(h, w_qkv, positions)
    b, hkv, g, s, hd = q.shape
    outs, lses = [], []
    for window, r in DILATED_BRANCHES:
        half = window // (2 * r)
        L = s // r
        qd = q.reshape(b, hkv, g, L, r, hd).transpose(0, 4, 1, 2, 3, 5).reshape(b * r, hkv, g, L, hd)
        kd = k.reshape(b, hkv, L, r, hd).transpose(0, 3, 1, 2, 4).reshape(b * r, hkv, L, hd)
        vd = v.reshape(b, hkv, L, r, hd).transpose(0, 3, 1, 2, 4).reshape(b * r, hkv, L, hd)
        o, lse = banded_attention(qd, kd, vd, half)
        outs.append(o.reshape(b, r, hkv, g, L, hd).transpose(0, 2, 3, 4, 1, 5).reshape(b, hkv, g, s, hd))
        lses.append(lse.reshape(b, r, hkv, g, L).transpose(0, 2, 3, 4, 1).reshape(b, hkv, g, s))
    alpha = jax.nn.softmax(jnp.stack(lses, axis=0), axis=0)
    o = sum(alpha[i][..., None] * outs[i].astype(jnp.float32) for i in range(len(outs)))
    return merge_heads(o.astype(h.dtype), w_o)


def swiglu(h, w_gate_up, w_down):
    gu = h @ w_gate_up
    gate, up = gu[..., :D_FF], gu[..., D_FF:]
    return (jax.nn.silu(gate) * up) @ w_down


def sandwich_sublayer(x, fn, g_pre, g_post, shift, scale, gate):
    h = rms_norm(x, g_pre) * (1.0 + scale[:, None, :]) + shift[:, None, :]
    y = rms_norm(fn(h), g_post)
    return x + gate[:, None, :] * y


def setup_inputs(seed: int = 0) -> dict:
    key = jax.random.key(seed)
    ks = jax.random.split(key, 16)
    f32 = jnp.float32
    x = jax.random.normal(ks[0], (BATCH, SEQ, D_MODEL), f32)
    c = jax.random.normal(ks[1], (BATCH, D_MODEL), f32)
    positions = jnp.broadcast_to(jnp.arange(SEQ, dtype=jnp.int32)[None, :], (BATCH, SEQ))
    w_qkv = jax.random.normal(ks[2], (DEPTH, D_MODEL, QKV_DIM), f32) * D_MODEL ** -0.5
    w_o = jax.random.normal(ks[3], (DEPTH, Q_DIM, D_MODEL), f32) * Q_DIM ** -0.5
    sink = jax.random.normal(ks[4], (N_A_LAYERS, N_HEADS), f32)
    w_gate_up = jax.random.normal(ks[5], (DEPTH, D_MODEL, 2 * D_FF), f32) * D_MODEL ** -0.5
    w_down = jax.random.normal(ks[6], (DEPTH, D_FF, D_MODEL), f32) * D_FF ** -0.5
    g_mix_pre = 1.0 + 0.05 * jax.random.normal(ks[7], (DEPTH, D_MODEL), f32)
    g_mix_post = 1.0 + 0.05 * jax.random.normal(ks[8], (DEPTH, D_MODEL), f32)
    g_ffn_pre = 1.0 + 0.05 * jax.random.normal(ks[9], (DEPTH, D_MODEL), f32)
    g_ffn_post = 1.0 + 0.05 * jax.random.normal(ks[10], (DEPTH, D_MODEL), f32)
    w_ada = jax.random.normal(ks[11], (DEPTH, D_MODEL, 6 * D_MODEL), f32) * (0.5 * D_MODEL ** -0.5)
    b_ada = 0.02 * jax.random.normal(ks[12], (DEPTH, 6 * D_MODEL), f32)
    return {"x": x, "c": c, "positions": positions, "w_qkv": w_qkv, "w_o": w_o, "sink": sink,
            "w_gate_up": w_gate_up, "w_down": w_down, "g_mix_pre": g_mix_pre,
            "g_mix_post": g_mix_post, "g_ffn_pre": g_ffn_pre, "g_ffn_post": g_ffn_post,
            "w_ada": w_ada, "b_ada": b_ada}


def reference(x, c, positions, w_qkv, w_o, sink, w_gate_up, w_down, g_mix_pre, g_mix_post,
              g_ffn_pre, g_ffn_post, w_ada, b_ada):
    for i in range(DEPTH):
        mod = c @ w_ada[i] + b_ada[i]
        sh_t, sc_t, gt_t, sh_f, sc_f, gt_f = jnp.split(mod, 6, axis=-1)
        if i % N_MIXERS == 0:
            a_idx = i // N_MIXERS
            mix = lambda h, i=i, a_idx=a_idx: mixer_window_sink(h, w_qkv[i], w_o[i], sink[a_idx], positions)
        else:
            mix = lambda h, i=i: mixer_dilated(h, w_qkv[i], w_o[i], positions)
        x = sandwich_sublayer(x, mix, g_mix_pre[i], g_mix_post[i], sh_t, sc_t, gt_t)
        ffn = lambda h, i=i: swiglu(h, w_gate_up[i], w_down[i])
        x = sandwich_sublayer(x, ffn, g_ffn_pre[i], g_ffn_post[i], sh_f, sc_f, gt_f)
    return x
```

```python
import functools
import math

import jax
import jax.numpy as jnp
from jax import lax
from jax.experimental import pallas as pl
from jax.experimental.pallas import tpu as pltpu

HEAD_DIM = 128
GROUP = 4
ROPE_THETA = 10000.0
WINDOW_A_HALF = 128
DILATED_BRANCHES = ((128, 1), (512, 4), (2048, 16))
RMS_EPS = 1e-6
NEG_INF = -1e30
N_MIXERS = 2

LANES = 128
VMEM_LIMIT = 58 * 1024 * 1024

F32 = jnp.float32
BF16 = jnp.bfloat16


def _params(sem, vmem=VMEM_LIMIT):
    return pltpu.CompilerParams(dimension_semantics=sem, vmem_limit_bytes=vmem)


def _cast_kernel(w_ref, o_ref):
    o_ref[...] = w_ref[...].astype(o_ref.dtype)


def cast_weights(w, bk, bn):
    nl, k, n = w.shape
    spec = pl.BlockSpec((None, bk, bn), lambda l, i, j: (l, i, j))
    return pl.pallas_call(
        _cast_kernel,
        out_shape=jax.ShapeDtypeStruct(w.shape, BF16),
        grid=(nl, k // bk, n // bn),
        in_specs=[spec],
        out_specs=spec,
        compiler_params=_params(("parallel", "parallel", "parallel")),
        name="cast_weights",
    )(w)


def _ada_kernel(c_ref, w_ref, b_ref, o_ref):
    c = c_ref[...].astype(BF16)
    w = w_ref[...].astype(BF16)
    o_ref[...] = jnp.dot(c, w, preferred_element_type=F32) + b_ref[...]


def ada_modulation(c_pad, w_ada, b_ada, bn=512):
    nl, d, n = w_ada.shape
    rows = c_pad.shape[0]
    return pl.pallas_call(
        _ada_kernel,
        out_shape=jax.ShapeDtypeStruct((nl, rows, n), F32),
        grid=(nl, n // bn),
        in_specs=[
            pl.BlockSpec((rows, d), lambda l, j: (0, 0)),
            pl.BlockSpec((None, d, bn), lambda l, j: (l, 0, j)),
            pl.BlockSpec((None, 1, bn), lambda l, j: (l, 0, j)),
        ],
        out_specs=pl.BlockSpec((None, rows, bn), lambda l, j: (l, 0, j)),
        compiler_params=_params(("parallel", "parallel")),
        name="ada_modulation",
    )(c_pad, w_ada, b_ada.reshape(nl, 1, n))


def _rope_table_kernel(pos_ref, cos_ref, sin_ref):
    half = HEAD_DIM // 2
    pos = pos_ref[...].astype(F32)
    lane = lax.broadcasted_iota(jnp.int32, (1, HEAD_DIM), 1)
    fidx = (lane & (half - 1)).astype(F32)
    inv_freq = jnp.exp(-math.log(ROPE_THETA) * fidx / half)
    ang = pos * inv_freq
    cos_ref[...] = jnp.cos(ang)
    s = jnp.sin(ang)
    sin_ref[...] = jnp.where(lane < half, -s, s)


def rope_tables(pos_col, bm=1024):
    t = pos_col.shape[0]
    out = jax.ShapeDtypeStruct((t, HEAD_DIM), F32)
    return pl.pallas_call(
        _rope_table_kernel,
        out_shape=(out, out),
        grid=(t // bm,),
        in_specs=[pl.BlockSpec((bm, 1), lambda i: (i, 0))],
        out_specs=(pl.BlockSpec((bm, HEAD_DIM), lambda i: (i, 0)),
                   pl.BlockSpec((bm, HEAD_DIM), lambda i: (i, 0))),
        compiler_params=_params(("parallel",)),
        name="rope_tables",
    )(pos_col)


def _rms(x, g):
    ms = jnp.mean(x * x, axis=-1, keepdims=True)
    return x * lax.rsqrt(ms + RMS_EPS) * g


def _prenorm_kernel(x_ref, g_ref, sc_ref, sh_ref, h_ref):
    y = _rms(x_ref[...], g_ref[...])
    h_ref[...] = (y * (1.0 + sc_ref[...]) + sh_ref[...]).astype(h_ref.dtype)


def _mod_spec(layer, chunk, d):
    return pl.BlockSpec((None, None, 1, d), lambda b, i: (layer, b, 0, chunk))


def _vec_spec(layer, d):
    return pl.BlockSpec((None, 1, d), lambda b, i: (layer, 0, 0))


def prenorm(x, g3, mod4, layer, sc_chunk, sh_chunk, bm=256):
    nb, s, d = x.shape
    row = pl.BlockSpec((None, bm, d), lambda b, i: (b, i, 0))
    return pl.pallas_call(
        _prenorm_kernel,
        out_shape=jax.ShapeDtypeStruct(x.shape, BF16),
        grid=(nb, s // bm),
        in_specs=[row, _vec_spec(layer, d), _mod_spec(layer, sc_chunk, d), _mod_spec(layer, sh_chunk, d)],
        out_specs=row,
        compiler_params=_params(("parallel", "parallel")),
        name="prenorm",
    )(x, g3, mod4, mod4)


def _post_kernel(x_ref, y_ref, gpost_ref, gate_ref, xo_ref):
    yn = _rms(y_ref[...].astype(F32), gpost_ref[...])
    xo_ref[...] = x_ref[...] + gate_ref[...] * yn


def _post_next_kernel(x_ref, y_ref, gpost_ref, gate_ref, gpre_ref, sc_ref, sh_ref, xo_ref, h_ref):
    yn = _rms(y_ref[...].astype(F32), gpost_ref[...])
    xn = x_ref[...] + gate_ref[...] * yn
    xo_ref[...] = xn
    hn = _rms(xn, gpre_ref[...])
    h_ref[...] = (hn * (1.0 + sc_ref[...]) + sh_ref[...]).astype(h_ref.dtype)


def post_residual(x, y, gpost3, mod4, layer, gate_chunk, nxt=None, bm=256):
    nb, s, d = x.shape
    row = pl.BlockSpec((None, bm, d), lambda b, i: (b, i, 0))
    in_specs = [row, row, _vec_spec(layer, d), _mod_spec(layer, gate_chunk, d)]
    args = [x, y, gpost3, mod4]
    if nxt is None:
        return pl.pallas_call(
            _post_kernel,
            out_shape=jax.ShapeDtypeStruct(x.shape, F32),
            grid=(nb, s // bm),
            in_specs=in_specs,
            out_specs=row,
            compiler_params=_params(("parallel", "parallel")),
            name="post_residual",
        )(*args), None
    g3n, ln, scn, shn = nxt
    in_specs += [_vec_spec(ln, d), _mod_spec(ln, scn, d), _mod_spec(ln, shn, d)]
    args += [g3n, mod4, mod4]
    return pl.pallas_call(
        _post_next_kernel,
        out_shape=(jax.ShapeDtypeStruct(x.shape, F32), jax.ShapeDtypeStruct(x.shape, BF16)),
        grid=(nb, s // bm),
        in_specs=in_specs,
        out_specs=(row, row),
        compiler_params=_params(("parallel", "parallel")),
        name="post_residual_next",
    )(*args)


def _mm_kernel(a_ref, w_ref, o_ref):
    o_ref[...] = jnp.dot(a_ref[...], w_ref[...], preferred_element_type=F32).astype(o_ref.dtype)


def matmul(a, w3, layer, bm, bn, out_dtype=BF16, name="matmul"):
    m, k = a.shape
    n = w3.shape[2]
    return pl.pallas_call(
        _mm_kernel,
        out_shape=jax.ShapeDtypeStruct((m, n), out_dtype),
        grid=(m // bm, n // bn),
        in_specs=[pl.BlockSpec((bm, k), lambda i, j: (i, 0)),
                  pl.BlockSpec((None, k, bn), lambda i, j: (layer, 0, j))],
        out_specs=pl.BlockSpec((bm, bn), lambda i, j: (i, j)),
        compiler_params=_params(("parallel", "arbitrary")),
        name=name,
    )(a, w3)


def _qkv_kernel(a_ref, w_ref, cos_ref, sin_ref, o_ref, *, n_q_blocks, n_rope_blocks, q_scale):
    j = pl.program_id(1)
    acc = jnp.dot(a_ref[...], w_ref[...], preferred_element_type=F32)
    n_heads = acc.shape[1] // HEAD_DIM

    @pl.when(j < n_rope_blocks)
    def _():
        mult = jnp.where(j < n_q_blocks, q_scale, 1.0).astype(F32)
        cos = cos_ref[...] * mult
        sin = sin_ref[...] * mult
        for h in range(n_heads):
            t = acc[:, h * HEAD_DIM:(h + 1) * HEAD_DIM]
            r = pltpu.roll(t, HEAD_DIM // 2, 1)
            o_ref[:, h * HEAD_DIM:(h + 1) * HEAD_DIM] = (t * cos + r * sin).astype(o_ref.dtype)

    @pl.when(j >= n_rope_blocks)
    def _():
        o_ref[...] = acc.astype(o_ref.dtype)


def qkv_projection(h2, w3, layer, cos, sin, q_dim, kv_dim, out_dtype, bm=1024, bn=1024):
    m, k = h2.shape
    n = w3.shape[2]
    kern = functools.partial(_qkv_kernel, n_q_blocks=q_dim // bn, n_rope_blocks=(q_dim + kv_dim) // bn,
                             q_scale=1.0 / math.sqrt(HEAD_DIM))
    return pl.pallas_call(
        kern,
        out_shape=jax.ShapeDtypeStruct((m, n), out_dtype),
        grid=(m // bm, n // bn),
        in_specs=[pl.BlockSpec((bm, k), lambda i, j: (i, 0)),
                  pl.BlockSpec((None, k, bn), lambda i, j: (layer, 0, j)),
                  pl.BlockSpec((bm, HEAD_DIM), lambda i, j: (i, 0)),
                  pl.BlockSpec((bm, HEAD_DIM), lambda i, j: (i, 0))],
        out_specs=pl.BlockSpec((bm, bn), lambda i, j: (i, j)),
        compiler_params=_params(("parallel", "arbitrary")),
        name="qkv_projection",
    )(h2, w3, cos, sin)


def _gate_up_kernel(a_ref, wg_ref, wu_ref, o_ref):
    a = a_ref[...]
    g = jnp.dot(a, wg_ref[...], preferred_element_type=F32)
    u = jnp.dot(a, wu_ref[...], preferred_element_type=F32)
    o_ref[...] = (g * jax.nn.sigmoid(g) * u).astype(o_ref.dtype)


def gate_up(h2, w3, layer, d_ff, bm=2048, bn=256):
    m, k = h2.shape
    nblk = d_ff // bn
    return pl.pallas_call(
        _gate_up_kernel,
        out_shape=jax.ShapeDtypeStruct((m, d_ff), BF16),
        grid=(m // bm, nblk),
        in_specs=[pl.BlockSpec((bm, k), lambda i, j: (i, 0)),
                  pl.BlockSpec((None, k, bn), lambda i, j: (layer, 0, j)),
                  pl.BlockSpec((None, k, bn), lambda i, j: (layer, 0, j + nblk))],
        out_specs=pl.BlockSpec((bm, bn), lambda i, j: (i, j)),
        compiler_params=_params(("parallel", "arbitrary")),
        name="gate_up",
    )(h2, w3, w3)


def _attn_kernel(*refs, half_w, tq, tk, seq_len, has_sink, want_lse):
    refs = list(refs)
    sink_ref = refs.pop(0) if has_sink else None
    q_ref, k_ref, v_ref, o_ref = refs[:4]
    lse_ref = refs[4] if want_lse else None

    h = pl.program_id(1)
    q0 = pl.program_id(2) * tq
    start = pl.multiple_of(jnp.clip(q0 - half_w, 0, seq_len - tk), 16)
    k = k_ref[pl.ds(start, tk), :]
    v = v_ref[pl.ds(start, tk), :]
    q = q_ref[...]
    qs = jnp.concatenate([q[:, g * HEAD_DIM:(g + 1) * HEAD_DIM] for g in range(GROUP)], axis=0)
    s = lax.dot_general(qs, k, (((1,), (1,)), ((), ())), preferred_element_type=F32)

    qpos = q0 + lax.broadcasted_iota(jnp.int32, (tq, tk), 0)
    kpos = start + lax.broadcasted_iota(jnp.int32, (tq, tk), 1)
    valid = jnp.abs(qpos - kpos) <= half_w
    s = jnp.where(valid[None], s.reshape(GROUP, tq, tk), NEG_INF)
    m = jnp.max(s, axis=-1, keepdims=True)
    if has_sink:
        sink = jnp.stack([jnp.full((tq, 1), sink_ref[0, h * GROUP + g], F32) for g in range(GROUP)], axis=0)
        m = jnp.maximum(m, sink)
    p = jnp.exp(s - m)
    denom = jnp.sum(p, axis=-1, keepdims=True)
    if has_sink:
        denom = denom + jnp.exp(sink - m)
    pv = jnp.dot(p.reshape(GROUP * tq, tk).astype(BF16), v, preferred_element_type=F32)
    o = pv.reshape(GROUP, tq, HEAD_DIM) / denom
    for g in range(GROUP):
        o_ref[:, g * HEAD_DIM:(g + 1) * HEAD_DIM] = o[g].astype(o_ref.dtype)
    if want_lse:
        lse = m + jnp.log(denom)
        lane = lax.broadcasted_iota(jnp.int32, (tq, LANES), 1)
        tile = jnp.zeros((tq, LANES), F32)
        for g in range(GROUP):
            tile = jnp.where(lane == g, lse[g], tile)
        lse_ref[...] = tile


def banded_attention(qkv, half_w, n_kv_heads, sink=None, out_dtype=BF16, want_lse=False, tq=256):
    nseq, seq_len, _ = qkv.shape
    tq = min(tq, seq_len)
    tk = min(tq + 2 * half_w, seq_len)
    n_q_cols = n_kv_heads * GROUP
    kern = functools.partial(_attn_kernel, half_w=half_w, tq=tq, tk=tk, seq_len=seq_len,
                             has_sink=sink is not None, want_lse=want_lse)
    in_specs = [
        pl.BlockSpec((None, tq, GROUP * HEAD_DIM), lambda s, h, t: (s, t, h)),
        pl.BlockSpec((None, seq_len, HEAD_DIM), lambda s, h, t: (s, 0, n_q_cols + h)),
        pl.BlockSpec((None, seq_len, HEAD_DIM), lambda s, h, t: (s, 0, n_q_cols + n_kv_heads + h)),
    ]
    args = [qkv, qkv, qkv]
    if sink is not None:
        in_specs.insert(0, pl.BlockSpec(memory_space=pltpu.SMEM))
        args.insert(0, sink.reshape(1, -1).astype(F32))
    o_shape = jax.ShapeDtypeStruct((nseq, seq_len, n_q_cols * HEAD_DIM), out_dtype)
    o_spec = pl.BlockSpec((None, tq, GROUP * HEAD_DIM), lambda s, h, t: (s, t, h))
    if want_lse:
        out_shape = (o_shape, jax.ShapeDtypeStruct((nseq, seq_len, n_kv_heads * LANES), F32))
        out_specs = (o_spec, pl.BlockSpec((None, tq, LANES), lambda s, h, t: (s, t, h)))
    else:
        out_shape, out_specs = o_shape, o_spec
    return pl.pallas_call(
        kern,
        out_shape=out_shape,
        grid=(nseq, n_kv_heads, seq_len // tq),
        in_specs=in_specs,
        out_specs=out_specs,
        compiler_params=_params(("parallel", "parallel", "arbitrary")),
        name=f"banded_attention_w{half_w}_l{seq_len}",
    )(*args)


SUPER = 1024


def _dilate_kernel(x_ref, nat_ref, r4_ref, r16_ref):
    nat_ref[...] = x_ref[...].astype(nat_ref.dtype)
    for res in range(4):
        r4_ref[res] = x_ref[pl.ds(res, SUPER // 4, stride=4), :].astype(r4_ref.dtype)
    for res in range(16):
        r16_ref[res] = x_ref[pl.ds(res, SUPER // 16, stride=16), :].astype(r16_ref.dtype)


def dilate(qkv):
    nb, s, c = qkv.shape
    return pl.pallas_call(
        _dilate_kernel,
        out_shape=(jax.ShapeDtypeStruct((nb, s, c), BF16),
                   jax.ShapeDtypeStruct((nb, 4, s // 4, c), BF16),
                   jax.ShapeDtypeStruct((nb, 16, s // 16, c), BF16)),
        grid=(nb, s // SUPER, c // LANES),
        in_specs=[pl.BlockSpec((None, SUPER, LANES), lambda b, i, j: (b, i, j))],
        out_specs=(pl.BlockSpec((None, SUPER, LANES), lambda b, i, j: (b, i, j)),
                   pl.BlockSpec((None, 4, SUPER // 4, LANES), lambda b, i, j: (b, 0, i, j)),
                   pl.BlockSpec((None, 16, SUPER // 16, LANES), lambda b, i, j: (b, 0, i, j))),
        compiler_params=_params(("parallel", "parallel", "parallel")),
        name="dilate",
    )(qkv)


def _combine_kernel(o1_ref, l1_ref, o4_ref, l4_ref, o16_ref, l16_ref, out_ref, s4, s16, t4, t16):
    for res in range(4):
        t4[pl.ds(res, SUPER // 4, stride=4), :] = l4_ref[res]
    for res in range(16):
        t16[pl.ds(res, SUPER // 16, stride=16), :] = l16_ref[res]
    for g in range(GROUP):
        cols = slice(g * HEAD_DIM, (g + 1) * HEAD_DIM)
        for res in range(4):
            s4[pl.ds(res, SUPER // 4, stride=4), :] = o4_ref[res, :, cols]
        for res in range(16):
            s16[pl.ds(res, SUPER // 16, stride=16), :] = o16_ref[res, :, cols]
        a1 = l1_ref[:, g:g + 1]
        a4 = t4[:, g:g + 1]
        a16 = t16[:, g:g + 1]
        mx = jnp.maximum(jnp.maximum(a1, a4), a16)
        w1 = jnp.exp(a1 - mx)
        w4 = jnp.exp(a4 - mx)
        w16 = jnp.exp(a16 - mx)
        den = w1 + w4 + w16
        o = (w1 * o1_ref[:, cols] + w4 * s4[...] + w16 * s16[...]) / den
        out_ref[:, cols] = o.astype(out_ref.dtype)


def combine_branches(o1, l1, o4, l4, o16, l16, n_kv_heads):
    nb, s, c = o1.shape
    gw = GROUP * HEAD_DIM
    return pl.pallas_call(
        _combine_kernel,
        out_shape=jax.ShapeDtypeStruct((nb, s, c), BF16),
        grid=(nb, s // SUPER, n_kv_heads),
        in_specs=[pl.BlockSpec((None, SUPER, gw), lambda b, i, h: (b, i, h)),
                  pl.BlockSpec((None, SUPER, LANES), lambda b, i, h: (b, i, h)),
                  pl.BlockSpec((None, 4, SUPER // 4, gw), lambda b, i, h: (b, 0, i, h)),
                  pl.BlockSpec((None, 4, SUPER // 4, LANES), lambda b, i, h: (b, 0, i, h)),
                  pl.BlockSpec((None, 16, SUPER // 16, gw), lambda b, i, h: (b, 0, i, h)),
                  pl.BlockSpec((None, 16, SUPER // 16, LANES), lambda b, i, h: (b, 0, i, h))],
        out_specs=pl.BlockSpec((None, SUPER, gw), lambda b, i, h: (b, i, h)),
        scratch_shapes=[pltpu.VMEM((SUPER, LANES), F32) for _ in range(4)],
        compiler_params=_params(("parallel", "parallel", "arbitrary")),
        name="combine_branches",
    )(o1, l1, o4, l4, o16, l16)


def kernel(x, c, positions, w_qkv, w_o, sink, w_gate_up, w_down, g_mix_pre, g_mix_post,
           g_ffn_pre, g_ffn_post, w_ada, b_ada):
    nb, s, d = x.shape
    depth = w_qkv.shape[0]
    t = nb * s
    q_dim = w_o.shape[1]
    kv_dim = (w_qkv.shape[2] - q_dim) // 2
    n_kv_heads = kv_dim // HEAD_DIM
    d_ff = w_down.shape[1]
    assert [r for _, r in DILATED_BRANCHES] == [1, 4, 16] and all(w // (2 * r) == 64 for w, r in DILATED_BRANCHES)

    wqkv_b = cast_weights(w_qkv, bk=1024, bn=512)
    wo_b = cast_weights(w_o, bk=1024, bn=512)
    wgu_b = cast_weights(w_gate_up, bk=1024, bn=512)
    wd_b = cast_weights(w_down, bk=d_ff // 4, bn=512)

    c_pad = jnp.pad(c, ((0, 8 - nb), (0, 0)))
    mod = ada_modulation(c_pad, w_ada, b_ada)
    mod4 = mod.reshape(depth, 8, 1, 6 * d)
    cos, sin = rope_tables(positions.reshape(t, 1))

    g_mix_pre3 = g_mix_pre.reshape(depth, 1, d)
    g_mix_post3 = g_mix_post.reshape(depth, 1, d)
    g_ffn_pre3 = g_ffn_pre.reshape(depth, 1, d)
    g_ffn_post3 = g_ffn_post.reshape(depth, 1, d)
    SH_T, SC_T, GT_T, SH_F, SC_F, GT_F = range(6)

    h = prenorm(x, g_mix_pre3, mod4, 0, SC_T, SH_T)
    for i in range(depth):
        h2 = h.reshape(t, d)
        if i % N_MIXERS == 0:
            qkv = qkv_projection(h2, wqkv_b, i, cos, sin, q_dim, kv_dim, BF16)
            att = banded_attention(qkv.reshape(nb, s, -1), WINDOW_A_HALF, n_kv_heads,
                                   sink=sink[i // N_MIXERS])
        else:
            qkv = qkv_projection(h2, wqkv_b, i, cos, sin, q_dim, kv_dim, F32)
            nat, r4, r16 = dilate(qkv.reshape(nb, s, -1))
            o1, l1 = banded_attention(nat, 64, n_kv_heads, out_dtype=F32, want_lse=True)
            o4, l4 = banded_attention(r4.reshape(nb * 4, s // 4, -1), 64, n_kv_heads,
                                      out_dtype=F32, want_lse=True)
            o16, l16 = banded_attention(r16.reshape(nb * 16, s // 16, -1), 64, n_kv_heads,
                                        out_dtype=F32, want_lse=True)
            att = combine_branches(o1, l1,
                                   o4.reshape(nb, 4, s // 4, -1), l4.reshape(nb, 4, s // 4, -1),
                                   o16.reshape(nb, 16, s // 16, -1), l16.reshape(nb, 16, s // 16, -1),
                                   n_kv_heads)
        y = matmul(att.reshape(t, q_dim), wo_b, i, bm=1024, bn=1024, name="wo_projection")
        x, h = post_residual(x, y.reshape(nb, s, d), g_mix_post3, mod4, i, GT_T,
                             nxt=(g_ffn_pre3, i, SC_F, SH_F))
        a = gate_up(h.reshape(t, d), wgu_b, i, d_ff)
        y = matmul(a, wd_b, i, bm=512, bn=512, name="down_projection")
        nxt = (g_mix_pre3, i + 1, SC_T, SH_T) if i + 1 < depth else None
        x, h = post_residual(x, y.reshape(nb, s, d), g_ffn_post3, mod4, i, GT_F, nxt=nxt)
    return x
```

```python
import functools
import math

import jax
import jax.numpy as jnp
from jax import lax
from jax.experimental import pallas as pl
from jax.experimental.pallas import tpu as pltpu

HEAD_DIM = 128
GROUP = 4
ROPE_THETA = 10000.0
WINDOW_A_HALF = 128
DILATED_BRANCHES = ((128, 1), (512, 4), (2048, 16))
RMS_EPS = 1e-6
NEG_INF = -1e30
N_MIXERS = 2
LOG2E = 1.4426950408889634
LN2 = 0.6931471805599453

LANES = 128
VMEM_LIMIT = 58 * 1024 * 1024

F32 = jnp.float32
BF16 = jnp.bfloat16


def _params(sem, vmem=VMEM_LIMIT):
    return pltpu.CompilerParams(dimension_semantics=sem, vmem_limit_bytes=vmem)


def _ada_kernel(c_ref, w_ref, b_ref, o_ref):
    c = c_ref[...].astype(BF16)
    w = w_ref[...].astype(BF16)
    o_ref[...] = jnp.dot(c, w, preferred_element_type=F32) + b_ref[...]


def ada_modulation(c_pad, w_ada, b_ada, bn=512):
    nl, d, n = w_ada.shape
    rows = c_pad.shape[0]
    return pl.pallas_call(
        _ada_kernel,
        out_shape=jax.ShapeDtypeStruct((nl, rows, n), F32),
        grid=(nl, n // bn),
        in_specs=[
            pl.BlockSpec((rows, d), lambda l, j: (0, 0)),
            pl.BlockSpec((None, d, bn), lambda l, j: (l, 0, j)),
            pl.BlockSpec((None, 1, bn), lambda l, j: (l, 0, j)),
        ],
        out_specs=pl.BlockSpec((None, rows, bn), lambda l, j: (l, 0, j)),
        compiler_params=_params(("parallel", "parallel")),
        name="ada_modulation",
    )(c_pad, w_ada, b_ada.reshape(nl, 1, n))


def _rope_table_kernel(pos_ref, cos_ref, sin_ref):
    half = HEAD_DIM // 2
    pos = pos_ref[...].astype(F32)
    lane = lax.broadcasted_iota(jnp.int32, (1, HEAD_DIM), 1)
    fidx = (lane & (half - 1)).astype(F32)
    inv_freq = jnp.exp(-math.log(ROPE_THETA) * fidx / half)
    ang = pos * inv_freq
    cos_ref[...] = jnp.cos(ang)
    s = jnp.sin(ang)
    sin_ref[...] = jnp.where(lane < half, -s, s)


def rope_tables(pos_col, bm=1024):
    t = pos_col.shape[0]
    out = jax.ShapeDtypeStruct((t, HEAD_DIM), F32)
    return pl.pallas_call(
        _rope_table_kernel,
        out_shape=(out, out),
        grid=(t // bm,),
        in_specs=[pl.BlockSpec((bm, 1), lambda i: (i, 0))],
        out_specs=(pl.BlockSpec((bm, HEAD_DIM), lambda i: (i, 0)),
                   pl.BlockSpec((bm, HEAD_DIM), lambda i: (i, 0))),
        compiler_params=_params(("parallel",)),
        name="rope_tables",
    )(pos_col)


def _rms(x, g):
    ms = jnp.mean(x * x, axis=-1, keepdims=True)
    return x * lax.rsqrt(ms + RMS_EPS) * g


def _prenorm_kernel(x_ref, g_ref, sc_ref, sh_ref, h_ref):
    y = _rms(x_ref[...], g_ref[...])
    h_ref[...] = (y * (1.0 + sc_ref[...]) + sh_ref[...]).astype(h_ref.dtype)


def _mod_spec(layer, chunk, d):
    return pl.BlockSpec((None, None, 1, d), lambda b, i: (layer, b, 0, chunk))


def _vec_spec(layer, d):
    return pl.BlockSpec((None, 1, d), lambda b, i: (layer, 0, 0))


def prenorm(x, g3, mod4, layer, sc_chunk, sh_chunk, bm=256):
    nb, s, d = x.shape
    row = pl.BlockSpec((None, bm, d), lambda b, i: (b, i, 0))
    return pl.pallas_call(
        _prenorm_kernel,
        out_shape=jax.ShapeDtypeStruct(x.shape, BF16),
        grid=(nb, s // bm),
        in_specs=[row, _vec_spec(layer, d), _mod_spec(layer, sc_chunk, d), _mod_spec(layer, sh_chunk, d)],
        out_specs=row,
        compiler_params=_params(("parallel", "parallel")),
        name="prenorm",
    )(x, g3, mod4, mod4)


def _post_kernel(x_ref, y_ref, gpost_ref, gate_ref, xo_ref):
    yn = _rms(y_ref[...].astype(F32), gpost_ref[...])
    xo_ref[...] = x_ref[...] + gate_ref[...] * yn


def _post_next_kernel(x_ref, y_ref, gpost_ref, gate_ref, gpre_ref, sc_ref, sh_ref, xo_ref, h_ref):
    yn = _rms(y_ref[...].astype(F32), gpost_ref[...])
    xn = x_ref[...] + gate_ref[...] * yn
    xo_ref[...] = xn
    hn = _rms(xn, gpre_ref[...])
    h_ref[...] = (hn * (1.0 + sc_ref[...]) + sh_ref[...]).astype(h_ref.dtype)


def post_residual(x, y, gpost3, mod4, layer, gate_chunk, nxt=None, bm=256):
    nb, s, d = x.shape
    row = pl.BlockSpec((None, bm, d), lambda b, i: (b, i, 0))
    in_specs = [row, row, _vec_spec(layer, d), _mod_spec(layer, gate_chunk, d)]
    args = [x, y, gpost3, mod4]
    if nxt is None:
        return pl.pallas_call(
            _post_kernel,
            out_shape=jax.ShapeDtypeStruct(x.shape, F32),
            grid=(nb, s // bm),
            in_specs=in_specs,
            out_specs=row,
            compiler_params=_params(("parallel", "parallel")),
            name="post_residual",
        )(*args), None
    g3n, ln, scn, shn = nxt
    in_specs += [_vec_spec(ln, d), _mod_spec(ln, scn, d), _mod_spec(ln, shn, d)]
    args += [g3n, mod4, mod4]
    return pl.pallas_call(
        _post_next_kernel,
        out_shape=(jax.ShapeDtypeStruct(x.shape, F32), jax.ShapeDtypeStruct(x.shape, BF16)),
        grid=(nb, s // bm),
        in_specs=in_specs,
        out_specs=(row, row),
        compiler_params=_params(("parallel", "parallel")),
        name="post_residual_next",
    )(*args)


def _act_spec(bm, k):
    return pl.BlockSpec((bm, k), lambda i, j: (i, 0), pipeline_mode=pl.Buffered(1))


def _mm_kernel(a_ref, w_ref, o_ref):
    w = w_ref[...].astype(BF16)
    o_ref[...] = jnp.dot(a_ref[...], w, preferred_element_type=F32).astype(o_ref.dtype)


def matmul_f32w(a, w3, layer, bm, bn, out_dtype=BF16, name="matmul"):
    m, k = a.shape
    n = w3.shape[2]
    return pl.pallas_call(
        _mm_kernel,
        out_shape=jax.ShapeDtypeStruct((m, n), out_dtype),
        grid=(m // bm, n // bn),
        in_specs=[_act_spec(bm, k),
                  pl.BlockSpec((None, k, bn), lambda i, j: (layer, 0, j))],
        out_specs=pl.BlockSpec((bm, bn), lambda i, j: (i, j)),
        compiler_params=_params(("arbitrary", "arbitrary")),
        name=name,
    )(a, w3)


def matmul_bf16w(a, w2, bm, bn, out_dtype=BF16, name="matmul"):
    m, k = a.shape
    n = w2.shape[1]
    return pl.pallas_call(
        _mm_kernel,
        out_shape=jax.ShapeDtypeStruct((m, n), out_dtype),
        grid=(m // bm, n // bn),
        in_specs=[pl.BlockSpec((bm, k), lambda i, j: (i, 0)),
                  pl.BlockSpec((k, bn), lambda i, j: (0, j))],
        out_specs=pl.BlockSpec((bm, bn), lambda i, j: (i, j)),
        compiler_params=_params(("arbitrary", "arbitrary")),
        name=name,
    )(a, w2)


def _qkv_kernel(a_ref, w_ref, cos_ref, sin_ref, o_ref, *, n_q_blocks, n_rope_blocks, q_scale):
    j = pl.program_id(1)
    acc = jnp.dot(a_ref[...], w_ref[...].astype(BF16), preferred_element_type=F32)
    n_heads = acc.shape[1] // HEAD_DIM

    @pl.when(j < n_rope_blocks)
    def _():
        mult = jnp.where(j < n_q_blocks, q_scale, 1.0).astype(F32)
        cos = cos_ref[...] * mult
        sin = sin_ref[...] * mult
        for h in range(n_heads):
            t = acc[:, h * HEAD_DIM:(h + 1) * HEAD_DIM]
            r = pltpu.roll(t, HEAD_DIM // 2, 1)
            o_ref[:, h * HEAD_DIM:(h + 1) * HEAD_DIM] = (t * cos + r * sin).astype(o_ref.dtype)

    @pl.when(j >= n_rope_blocks)
    def _():
        o_ref[...] = acc.astype(o_ref.dtype)


def qkv_projection(h2, w3, layer, cos, sin, q_dim, kv_dim, out_dtype, bm=2048, bn=512):
    m, k = h2.shape
    n = w3.shape[2]
    kern = functools.partial(_qkv_kernel, n_q_blocks=q_dim // bn, n_rope_blocks=(q_dim + kv_dim) // bn,
                             q_scale=LOG2E / math.sqrt(HEAD_DIM))
    tab = pl.BlockSpec((bm, HEAD_DIM), lambda i, j: (i, 0))
    return pl.pallas_call(
        kern,
        out_shape=jax.ShapeDtypeStruct((m, n), out_dtype),
        grid=(m // bm, n // bn),
        in_specs=[_act_spec(bm, k),
                  pl.BlockSpec((None, k, bn), lambda i, j: (layer, 0, j)),
                  tab, tab],
        out_specs=pl.BlockSpec((bm, bn), lambda i, j: (i, j)),
        compiler_params=_params(("arbitrary", "arbitrary")),
        name="qkv_projection",
    )(h2, w3, cos, sin)


def _gate_up_kernel(a_ref, wg_ref, wu_ref, wd_ref, o_ref, wdo_ref):
    a = a_ref[...]
    g = jnp.dot(a, wg_ref[...].astype(BF16), preferred_element_type=F32)
    u = jnp.dot(a, wu_ref[...].astype(BF16), preferred_element_type=F32)
    o_ref[...] = (g * jax.nn.sigmoid(g) * u).astype(o_ref.dtype)
    wdo_ref[...] = wd_ref[...].astype(wdo_ref.dtype)


def gate_up(h2, w3, w_down3, layer, d_ff, bm=2048, bn=256):
    m, k = h2.shape
    d_out = w_down3.shape[2]
    nblk = d_ff // bn
    steps = (m // bm) * nblk
    slab = d_ff // steps
    assert slab * steps == d_ff and slab % 16 == 0
    return pl.pallas_call(
        _gate_up_kernel,
        out_shape=(jax.ShapeDtypeStruct((m, d_ff), BF16), jax.ShapeDtypeStruct((d_ff, d_out), BF16)),
        grid=(m // bm, nblk),
        in_specs=[_act_spec(bm, k),
                  pl.BlockSpec((None, k, bn), lambda i, j: (layer, 0, j)),
                  pl.BlockSpec((None, k, bn), lambda i, j: (layer, 0, j + nblk)),
                  pl.BlockSpec((None, slab, d_out), lambda i, j: (layer, i * nblk + j, 0))],
        out_specs=(pl.BlockSpec((bm, bn), lambda i, j: (i, j)),
                   pl.BlockSpec((slab, d_out), lambda i, j: (i * nblk + j, 0))),
        compiler_params=_params(("arbitrary", "arbitrary")),
        name="gate_up",
    )(h2, w3, w3, w_down3)


def _attn_kernel(*refs, half_w, tq, tk, seq_len, n_seq, has_sink, want_lse):
    refs = list(refs)
    sink_ref = refs.pop(0) if has_sink else None
    q_ref, k_ref, v_ref, o_ref = refs[:4]
    lse_ref = refs[4] if want_lse else None

    h = pl.program_id(1)
    tiles_per_seq = seq_len // tq
    shift = tiles_per_seq.bit_length() - 1
    assert 1 << shift == tiles_per_seq
    rel = (lax.broadcasted_iota(jnp.int32, (tk, tq), 1)
           - lax.broadcasted_iota(jnp.int32, (tk, tq), 0))
    if has_sink:
        sinks = [sink_ref[0, h * GROUP + g] * LOG2E for g in range(GROUP)]

    def tile(it, carry):
        sq = it >> shift
        q0 = pl.multiple_of((it & (tiles_per_seq - 1)) * tq, tq)
        start = pl.multiple_of(jnp.clip(q0 - half_w, 0, seq_len - tk), 16)
        k = k_ref[sq, pl.ds(start, tk), :]
        v = v_ref[sq, pl.ds(start, tk), :]
        q = q_ref[sq, pl.ds(q0, tq), :]
        qs = jnp.concatenate([q[:, g * HEAD_DIM:(g + 1) * HEAD_DIM] for g in range(GROUP)], axis=0)
        st = lax.dot_general(k, qs, (((1,), (1,)), ((), ())), preferred_element_type=F32)
        dist = rel + (q0 - start)
        valid = (dist >= -half_w) & (dist <= half_w)
        ps, ms, dens = [], [], []
        for g in range(GROUP):
            sg = jnp.where(valid, st[:, g * tq:(g + 1) * tq], NEG_INF)
            m = jnp.max(sg, axis=0, keepdims=True)
            if has_sink:
                m = jnp.maximum(m, sinks[g])
            p = jnp.exp2(sg - m)
            den = jnp.sum(p, axis=0, keepdims=True)
            if has_sink:
                den = den + jnp.exp2(sinks[g] - m)
            ps.append(p.astype(BF16))
            ms.append(m)
            dens.append(den)
        pt = jnp.concatenate(ps, axis=1)
        ot = jnp.dot(v.T, pt, preferred_element_type=F32)
        for g in range(GROUP):
            og = ot[:, g * tq:(g + 1) * tq] / dens[g]
            o_ref[sq, pl.ds(q0, tq), g * HEAD_DIM:(g + 1) * HEAD_DIM] = og.T.astype(o_ref.dtype)
        if want_lse:
            rows = [(ms[g] + jnp.log2(dens[g])) * LN2 for g in range(GROUP)]
            rows.append(jnp.zeros((LANES - GROUP, tq), F32))
            lse_ref[sq, pl.ds(q0, tq), :] = jnp.concatenate(rows, axis=0).T
        return carry

    lax.fori_loop(0, n_seq * tiles_per_seq, tile, 0, unroll=2 if n_seq * tiles_per_seq >= 2 else 1)


def banded_attention(qkv, half_w, n_kv_heads, sink=None, out_dtype=BF16, want_lse=False,
                     tq=256, rows_per_step=4096):
    nseq, seq_len, _ = qkv.shape
    tq = min(tq, seq_len)
    tk = min(tq + 2 * half_w, seq_len)
    ns = max(1, min(nseq, rows_per_step // seq_len))
    n_q_cols = n_kv_heads * GROUP
    kern = functools.partial(_attn_kernel, half_w=half_w, tq=tq, tk=tk, seq_len=seq_len, n_seq=ns,
                             has_sink=sink is not None, want_lse=want_lse)
    in_specs = [
        pl.BlockSpec((ns, seq_len, GROUP * HEAD_DIM), lambda s, h: (s, 0, h)),
        pl.BlockSpec((ns, seq_len, HEAD_DIM), lambda s, h: (s, 0, n_q_cols + h)),
        pl.BlockSpec((ns, seq_len, HEAD_DIM), lambda s, h: (s, 0, n_q_cols + n_kv_heads + h)),
    ]
    args = [qkv, qkv, qkv]
    if sink is not None:
        in_specs.insert(0, pl.BlockSpec(memory_space=pltpu.SMEM))
        args.insert(0, sink.reshape(1, -1).astype(F32))
    o_shape = jax.ShapeDtypeStruct((nseq, seq_len, n_q_cols * HEAD_DIM), out_dtype)
    o_spec = pl.BlockSpec((ns, seq_len, GROUP * HEAD_DIM), lambda s, h: (s, 0, h))
    if want_lse:
        out_shape = (o_shape, jax.ShapeDtypeStruct((nseq, seq_len, n_kv_heads * LANES), F32))
        out_specs = (o_spec, pl.BlockSpec((ns, seq_len, LANES), lambda s, h: (s, 0, h)))
    else:
        out_shape, out_specs = o_shape, o_spec
    return pl.pallas_call(
        kern,
        out_shape=out_shape,
        grid=(nseq // ns, n_kv_heads),
        in_specs=in_specs,
        out_specs=out_specs,
        compiler_params=_params(("parallel", "parallel")),
        name=f"banded_attention_w{half_w}_l{seq_len}",
    )(*args)


SUPER = 1024


def _dilate_kernel(x_ref, nat_ref, r4_ref, r16_ref, slab):
    nat_ref[...] = x_ref[...].astype(nat_ref.dtype)
    for c in range(x_ref.shape[1] // LANES):
        cols = slice(c * LANES, (c + 1) * LANES)
        slab[...] = x_ref[:, cols]
        for res in range(4):
            r4_ref[res, :, cols] = slab[pl.ds(res, SUPER // 4, stride=4), :].astype(r4_ref.dtype)
        for res in range(16):
            r16_ref[res, :, cols] = slab[pl.ds(res, SUPER // 16, stride=16), :].astype(r16_ref.dtype)


def dilate(qkv, bw=512):
    nb, s, c = qkv.shape
    return pl.pallas_call(
        _dilate_kernel,
        out_shape=(jax.ShapeDtypeStruct((nb, s, c), BF16),
                   jax.ShapeDtypeStruct((nb, 4, s // 4, c), BF16),
                   jax.ShapeDtypeStruct((nb, 16, s // 16, c), BF16)),
        grid=(nb, s // SUPER, c // bw),
        in_specs=[pl.BlockSpec((None, SUPER, bw), lambda b, i, j: (b, i, j))],
        out_specs=(pl.BlockSpec((None, SUPER, bw), lambda b, i, j: (b, i, j)),
                   pl.BlockSpec((None, 4, SUPER // 4, bw), lambda b, i, j: (b, 0, i, j)),
                   pl.BlockSpec((None, 16, SUPER // 16, bw), lambda b, i, j: (b, 0, i, j))),
        scratch_shapes=[pltpu.VMEM((SUPER, LANES), F32)],
        compiler_params=_params(("parallel", "parallel", "parallel")),
        name="dilate",
    )(qkv)


def _combine_kernel(o1_ref, l1_ref, o4_ref, l4_ref, o16_ref, l16_ref, out_ref, s4, s16, t4, t16):
    for res in range(4):
        t4[pl.ds(res, SUPER // 4, stride=4), :] = l4_ref[res]
    for res in range(16):
        t16[pl.ds(res, SUPER // 16, stride=16), :] = l16_ref[res]
    for g in range(GROUP):
        cols = slice(g * HEAD_DIM, (g + 1) * HEAD_DIM)
        for res in range(4):
            s4[pl.ds(res, SUPER // 4, stride=4), :] = o4_ref[res, :, cols]
        for res in range(16):
            s16[pl.ds(res, SUPER // 16, stride=16), :] = o16_ref[res, :, cols]
        a1 = l1_ref[:, g:g + 1]
        a4 = t4[:, g:g + 1]
        a16 = t16[:, g:g + 1]
        mx = jnp.maximum(jnp.maximum(a1, a4), a16)
        w1 = jnp.exp(a1 - mx)
        w4 = jnp.exp(a4 - mx)
        w16 = jnp.exp(a16 - mx)
        den = w1 + w4 + w16
        o = (w1 * o1_ref[:, cols] + w4 * s4[...] + w16 * s16[...]) / den
        out_ref[:, cols] = o.astype(out_ref.dtype)


def combine_branches(o1, l1, o4, l4, o16, l16, n_kv_heads):
    nb, s, c = o1.shape
    gw = GROUP * HEAD_DIM
    return pl.pallas_call(
        _combine_kernel,
        out_shape=jax.ShapeDtypeStruct((nb, s, c), BF16),
        grid=(nb, s // SUPER, n_kv_heads),
        in_specs=[pl.BlockSpec((None, SUPER, gw), lambda b, i, h: (b, i, h)),
                  pl.BlockSpec((None, SUPER, LANES), lambda b, i, h: (b, i, h)),
                  pl.BlockSpec((None, 4, SUPER // 4, gw), lambda b, i, h: (b, 0, i, h)),
                  pl.BlockSpec((None, 4, SUPER // 4, LANES), lambda b, i, h: (b, 0, i, h)),
                  pl.BlockSpec((None, 16, SUPER // 16, gw), lambda b, i, h: (b, 0, i, h)),
                  pl.BlockSpec((None, 16, SUPER // 16, LANES), lambda b, i, h: (b, 0, i, h))],
        out_specs=pl.BlockSpec((None, SUPER, gw), lambda b, i, h: (b, i, h)),
        scratch_shapes=[pltpu.VMEM((SUPER, LANES), F32) for _ in range(4)],
        compiler_params=_params(("parallel", "parallel", "arbitrary")),
        name="combine_branches",
    )(o1, l1, o4, l4, o16, l16)


def kernel(x, c, positions, w_qkv, w_o, sink, w_gate_up, w_down, g_mix_pre, g_mix_post,
           g_ffn_pre, g_ffn_post, w_ada, b_ada):
    nb, s, d = x.shape
    depth = w_qkv.shape[0]
    t = nb * s
    q_dim = w_o.shape[1]
    kv_dim = (w_qkv.shape[2] - q_dim) // 2
    n_kv_heads = kv_dim // HEAD_DIM
    d_ff = w_down.shape[1]
    assert [r for _, r in DILATED_BRANCHES] == [1, 4, 16] and all(w // (2 * r) == 64 for w, r in DILATED_BRANCHES)

    c_pad = jnp.pad(c, ((0, 8 - nb), (0, 0)))
    mod = ada_modulation(c_pad, w_ada, b_ada)
    mod4 = mod.reshape(depth, 8, 1, 6 * d)
    cos, sin = rope_tables(positions.reshape(t, 1))

    g_mix_pre3 = g_mix_pre.reshape(depth, 1, d)
    g_mix_post3 = g_mix_post.reshape(depth, 1, d)
    g_ffn_pre3 = g_ffn_pre.reshape(depth, 1, d)
    g_ffn_post3 = g_ffn_post.reshape(depth, 1, d)
    SH_T, SC_T, GT_T, SH_F, SC_F, GT_F = range(6)

    h = prenorm(x, g_mix_pre3, mod4, 0, SC_T, SH_T)
    for i in range(depth):
        h2 = h.reshape(t, d)
        if i % N_MIXERS == 0:
            qkv = qkv_projection(h2, w_qkv, i, cos, sin, q_dim, kv_dim, BF16)
            att = banded_attention(qkv.reshape(nb, s, -1), WINDOW_A_HALF, n_kv_heads,
                                   sink=sink[i // N_MIXERS])
        else:
            qkv = qkv_projection(h2, w_qkv, i, cos, sin, q_dim, kv_dim, F32)
            nat, r4, r16 = dilate(qkv.reshape(nb, s, -1))
            o1, l1 = banded_attention(nat, 64, n_kv_heads, out_dtype=F32, want_lse=True)
            o4, l4 = banded_attention(r4.reshape(nb * 4, s // 4, -1), 64, n_kv_heads,
                                      out_dtype=F32, want_lse=True)
            o16, l16 = banded_attention(r16.reshape(nb * 16, s // 16, -1), 64, n_kv_heads,
                                        out_dtype=F32, want_lse=True)
            att = combine_branches(o1, l1,
                                   o4.reshape(nb, 4, s // 4, -1), l4.reshape(nb, 4, s // 4, -1),
                                   o16.reshape(nb, 16, s // 16, -1), l16.reshape(nb, 16, s // 16, -1),
                                   n_kv_heads)
        y = matmul_f32w(att.reshape(t, q_dim), w_o, i, bm=2048, bn=512, name="wo_projection")
        x, h = post_residual(x, y.reshape(nb, s, d), g_mix_post3, mod4, i, GT_T,
                             nxt=(g_ffn_pre3, i, SC_F, SH_F))
        a, wd_b = gate_up(h.reshape(t, d), w_gate_up, w_down, i, d_ff)
        y = matmul_bf16w(a, wd_b, bm=512, bn=512, name="down_projection")
        nxt = (g_mix_pre3, i + 1, SC_T, SH_T) if i + 1 < depth else None
        x, h = post_residual(x, y.reshape(nb, s, d), g_ffn_post3, mod4, i, GT_F, nxt=nxt)
    return x
```

```python
import functools
import math

import jax
import jax.numpy as jnp
from jax import lax
from jax.experimental import pallas as pl
from jax.experimental.pallas import tpu as pltpu

HEAD_DIM = 128
GROUP = 4
ROPE_THETA = 10000.0
WINDOW_A_HALF = 128
DILATED_BRANCHES = ((128, 1), (512, 4), (2048, 16))
RMS_EPS = 1e-6
NEG_INF = -1e30
N_MIXERS = 2
LOG2E = 1.4426950408889634
LN2 = 0.6931471805599453

LANES = 128
VMEM_LIMIT = 58 * 1024 * 1024

F32 = jnp.float32
BF16 = jnp.bfloat16


def _params(sem, vmem=VMEM_LIMIT):
    return pltpu.CompilerParams(dimension_semantics=sem, vmem_limit_bytes=vmem)


def _ada_kernel(c_ref, w_ref, b_ref, o_ref):
    c = c_ref[...].astype(BF16)
    w = w_ref[...].astype(BF16)
    o_ref[...] = jnp.dot(c, w, preferred_element_type=F32) + b_ref[...]


def ada_modulation(c_pad, w_ada, b_ada, bn=512):
    nl, d, n = w_ada.shape
    rows = c_pad.shape[0]
    return pl.pallas_call(
        _ada_kernel,
        out_shape=jax.ShapeDtypeStruct((nl, rows, n), F32),
        grid=(nl, n // bn),
        in_specs=[
            pl.BlockSpec((rows, d), lambda l, j: (0, 0)),
            pl.BlockSpec((None, d, bn), lambda l, j: (l, 0, j)),
            pl.BlockSpec((None, 1, bn), lambda l, j: (l, 0, j)),
        ],
        out_specs=pl.BlockSpec((None, rows, bn), lambda l, j: (l, 0, j)),
        compiler_params=_params(("parallel", "parallel")),
        name="ada_modulation",
    )(c_pad, w_ada, b_ada.reshape(nl, 1, n))


def _rope_table_kernel(pos_ref, cos_ref, sin_ref):
    half = HEAD_DIM // 2
    pos = pos_ref[...].astype(F32)
    lane = lax.broadcasted_iota(jnp.int32, (1, HEAD_DIM), 1)
    fidx = (lane & (half - 1)).astype(F32)
    inv_freq = jnp.exp(-math.log(ROPE_THETA) * fidx / half)
    ang = pos * inv_freq
    cos_ref[...] = jnp.cos(ang)
    s = jnp.sin(ang)
    sin_ref[...] = jnp.where(lane < half, -s, s)


def rope_tables(pos_col, bm=1024):
    t = pos_col.shape[0]
    out = jax.ShapeDtypeStruct((t, HEAD_DIM), F32)
    return pl.pallas_call(
        _rope_table_kernel,
        out_shape=(out, out),
        grid=(t // bm,),
        in_specs=[pl.BlockSpec((bm, 1), lambda i: (i, 0))],
        out_specs=(pl.BlockSpec((bm, HEAD_DIM), lambda i: (i, 0)),
                   pl.BlockSpec((bm, HEAD_DIM), lambda i: (i, 0))),
        compiler_params=_params(("parallel",)),
        name="rope_tables",
    )(pos_col)


def _rms(x, g):
    ms = jnp.mean(x * x, axis=-1, keepdims=True)
    return x * lax.rsqrt(ms + RMS_EPS) * g


def _prenorm_kernel(x_ref, g_ref, sc_ref, sh_ref, h_ref):
    y = _rms(x_ref[...], g_ref[...])
    h_ref[...] = (y * (1.0 + sc_ref[...]) + sh_ref[...]).astype(h_ref.dtype)


def _mod_spec(layer, chunk, d):
    return pl.BlockSpec((None, None, 1, d), lambda b, i: (layer, b, 0, chunk))


def _vec_spec(layer, d):
    return pl.BlockSpec((None, 1, d), lambda b, i: (layer, 0, 0))


def prenorm(x, g3, mod4, layer, sc_chunk, sh_chunk, bm=256):
    nb, s, d = x.shape
    row = pl.BlockSpec((None, bm, d), lambda b, i: (b, i, 0))
    return pl.pallas_call(
        _prenorm_kernel,
        out_shape=jax.ShapeDtypeStruct(x.shape, BF16),
        grid=(nb, s // bm),
        in_specs=[row, _vec_spec(layer, d), _mod_spec(layer, sc_chunk, d), _mod_spec(layer, sh_chunk, d)],
        out_specs=row,
        compiler_params=_params(("parallel", "parallel")),
        name="prenorm",
    )(x, g3, mod4, mod4)


def _post_kernel(x_ref, y_ref, gpost_ref, gate_ref, xo_ref):
    yn = _rms(y_ref[...].astype(F32), gpost_ref[...])
    xo_ref[...] = x_ref[...] + gate_ref[...] * yn


def _post_next_kernel(x_ref, y_ref, gpost_ref, gate_ref, gpre_ref, sc_ref, sh_ref, xo_ref, h_ref):
    yn = _rms(y_ref[...].astype(F32), gpost_ref[...])
    xn = x_ref[...] + gate_ref[...] * yn
    xo_ref[...] = xn
    hn = _rms(xn, gpre_ref[...])
    h_ref[...] = (hn * (1.0 + sc_ref[...]) + sh_ref[...]).astype(h_ref.dtype)


def post_residual(x, y, gpost3, mod4, layer, gate_chunk, nxt=None, bm=256):
    nb, s, d = x.shape
    row = pl.BlockSpec((None, bm, d), lambda b, i: (b, i, 0))
    in_specs = [row, row, _vec_spec(layer, d), _mod_spec(layer, gate_chunk, d)]
    args = [x, y, gpost3, mod4]
    if nxt is None:
        return pl.pallas_call(
            _post_kernel,
            out_shape=jax.ShapeDtypeStruct(x.shape, F32),
            grid=(nb, s // bm),
            in_specs=in_specs,
            out_specs=row,
            compiler_params=_params(("parallel", "parallel")),
            name="post_residual",
        )(*args), None
    g3n, ln, scn, shn = nxt
    in_specs += [_vec_spec(ln, d), _mod_spec(ln, scn, d), _mod_spec(ln, shn, d)]
    args += [g3n, mod4, mod4]
    return pl.pallas_call(
        _post_next_kernel,
        out_shape=(jax.ShapeDtypeStruct(x.shape, F32), jax.ShapeDtypeStruct(x.shape, BF16)),
        grid=(nb, s // bm),
        in_specs=in_specs,
        out_specs=(row, row),
        compiler_params=_params(("parallel", "parallel")),
        name="post_residual_next",
    )(*args)


def _act_spec(bm, k):
    return pl.BlockSpec((bm, k), lambda i, j: (i, 0), pipeline_mode=pl.Buffered(1))


def _mm_kernel(a_ref, w_ref, o_ref):
    w = w_ref[...].astype(BF16)
    o_ref[...] = jnp.dot(a_ref[...], w, preferred_element_type=F32).astype(o_ref.dtype)


def matmul_f32w(a, w3, layer, bm, bn, out_dtype=BF16, name="matmul"):
    m, k = a.shape
    n = w3.shape[2]
    return pl.pallas_call(
        _mm_kernel,
        out_shape=jax.ShapeDtypeStruct((m, n), out_dtype),
        grid=(m // bm, n // bn),
        in_specs=[_act_spec(bm, k),
                  pl.BlockSpec((None, k, bn), lambda i, j: (layer, 0, j))],
        out_specs=pl.BlockSpec((bm, bn), lambda i, j: (i, j)),
        compiler_params=_params(("arbitrary", "arbitrary")),
        name=name,
    )(a, w3)


def matmul_bf16w(a, w2, bm, bn, out_dtype=BF16, name="matmul"):
    m, k = a.shape
    n = w2.shape[1]
    return pl.pallas_call(
        _mm_kernel,
        out_shape=jax.ShapeDtypeStruct((m, n), out_dtype),
        grid=(m // bm, n // bn),
        in_specs=[pl.BlockSpec((bm, k), lambda i, j: (i, 0)),
                  pl.BlockSpec((k, bn), lambda i, j: (0, j))],
        out_specs=pl.BlockSpec((bm, bn), lambda i, j: (i, j)),
        compiler_params=_params(("arbitrary", "arbitrary")),
        name=name,
    )(a, w2)


def _qkv_kernel(a_ref, w_ref, cos_ref, sin_ref, o_ref, *, n_q_blocks, n_rope_blocks, q_scale):
    j = pl.program_id(1)
    w = w_ref[...].astype(BF16)
    is_rope = j < n_rope_blocks
    mult = jnp.where(j < n_q_blocks, q_scale, 1.0).astype(F32)
    n_heads = w.shape[1] // HEAD_DIM
    part = a_ref.shape[0] // 2
    for r in range(2):
        rows = slice(r * part, (r + 1) * part)
        acc = jnp.dot(a_ref[rows, :], w, preferred_element_type=F32)
        cos = jnp.where(is_rope, cos_ref[rows, :] * mult, 1.0)
        sin = jnp.where(is_rope, sin_ref[rows, :] * mult, 0.0)
        for h in range(n_heads):
            t = acc[:, h * HEAD_DIM:(h + 1) * HEAD_DIM]
            rot = pltpu.roll(t, HEAD_DIM // 2, 1)
            o_ref[rows, h * HEAD_DIM:(h + 1) * HEAD_DIM] = (t * cos + rot * sin).astype(o_ref.dtype)


def qkv_projection(h2, w3, layer, cos, sin, q_dim, kv_dim, bm=2048, bn=512):
    m, k = h2.shape
    n = w3.shape[2]
    kern = functools.partial(_qkv_kernel, n_q_blocks=q_dim // bn, n_rope_blocks=(q_dim + kv_dim) // bn,
                             q_scale=LOG2E / math.sqrt(HEAD_DIM))
    tab = pl.BlockSpec((bm, HEAD_DIM), lambda i, j: (i, 0))
    return pl.pallas_call(
        kern,
        out_shape=jax.ShapeDtypeStruct((m, n), BF16),
        grid=(m // bm, n // bn),
        in_specs=[_act_spec(bm, k),
                  pl.BlockSpec((None, k, bn), lambda i, j: (layer, 0, j)),
                  tab, tab],
        out_specs=pl.BlockSpec((bm, bn), lambda i, j: (i, j)),
        compiler_params=_params(("arbitrary", "arbitrary")),
        name="qkv_projection",
    )(h2, w3, cos, sin)


def _gate_up_kernel(a_ref, wg_ref, wu_ref, wd_ref, o_ref, wdo_ref):
    a = a_ref[...]
    g = jnp.dot(a, wg_ref[...].astype(BF16), preferred_element_type=F32)
    u = jnp.dot(a, wu_ref[...].astype(BF16), preferred_element_type=F32)
    o_ref[...] = (g * jax.nn.sigmoid(g) * u).astype(o_ref.dtype)
    wdo_ref[...] = wd_ref[...].astype(wdo_ref.dtype)


def gate_up(h2, w3, w_down3, layer, d_ff, bm=2048, bn=256):
    m, k = h2.shape
    d_out = w_down3.shape[2]
    nblk = d_ff // bn
    steps = (m // bm) * nblk
    slab = d_ff // steps
    assert slab * steps == d_ff and slab % 16 == 0
    return pl.pallas_call(
        _gate_up_kernel,
        out_shape=(jax.ShapeDtypeStruct((m, d_ff), BF16), jax.ShapeDtypeStruct((d_ff, d_out), BF16)),
        grid=(m // bm, nblk),
        in_specs=[_act_spec(bm, k),
                  pl.BlockSpec((None, k, bn), lambda i, j: (layer, 0, j)),
                  pl.BlockSpec((None, k, bn), lambda i, j: (layer, 0, j + nblk)),
                  pl.BlockSpec((None, slab, d_out), lambda i, j: (layer, i * nblk + j, 0))],
        out_specs=(pl.BlockSpec((bm, bn), lambda i, j: (i, j)),
                   pl.BlockSpec((slab, d_out), lambda i, j: (i * nblk + j, 0))),
        compiler_params=_params(("arbitrary", "arbitrary")),
        name="gate_up",
    )(h2, w3, w3, w_down3)


def _scores_t(k, qs):
    return lax.dot_general(k, qs, (((1,), (1,)), ((), ())), preferred_element_type=F32)


def _head_rows(q):
    return jnp.concatenate([q[:, g * HEAD_DIM:(g + 1) * HEAD_DIM] for g in range(GROUP)], axis=0)


def _window_sink_kernel(sink_ref, q_ref, k_ref, v_ref, o_ref, *, half_w, tq, tk, seq_len):
    h = pl.program_id(1)
    rel = (lax.broadcasted_iota(jnp.int32, (tk, tq), 1)
           - lax.broadcasted_iota(jnp.int32, (tk, tq), 0))
    sinks = [sink_ref[0, h * GROUP + g] * LOG2E for g in range(GROUP)]

    def tile(it, carry):
        q0 = pl.multiple_of(it * tq, tq)
        start = pl.multiple_of(jnp.clip(q0 - half_w, 0, seq_len - tk), 16)
        k = k_ref[pl.ds(start, tk), :]
        v = v_ref[pl.ds(start, tk), :]
        st = _scores_t(k, _head_rows(q_ref[pl.ds(q0, tq), :]))
        dist = rel + (q0 - start)
        valid = (dist >= -half_w) & (dist <= half_w)
        ps, dens = [], []
        for g in range(GROUP):
            sg = jnp.where(valid, st[:, g * tq:(g + 1) * tq], NEG_INF)
            m = jnp.maximum(jnp.max(sg, axis=0, keepdims=True), sinks[g])
            p = jnp.exp2(sg - m)
            dens.append(jnp.sum(p, axis=0, keepdims=True) + jnp.exp2(sinks[g] - m))
            ps.append(p.astype(BF16))
        ot = jnp.dot(v.T, jnp.concatenate(ps, axis=1), preferred_element_type=F32)
        for g in range(GROUP):
            og = ot[:, g * tq:(g + 1) * tq] / dens[g]
            o_ref[pl.ds(q0, tq), g * HEAD_DIM:(g + 1) * HEAD_DIM] = og.T.astype(o_ref.dtype)
        return carry

    lax.fori_loop(0, seq_len // tq, tile, 0, unroll=2)


def window_sink_attention(qkv, half_w, n_kv_heads, sink, tq=256):
    nb, seq_len, _ = qkv.shape
    tk = tq + 2 * half_w
    n_q_cols = n_kv_heads * GROUP
    kern = functools.partial(_window_sink_kernel, half_w=half_w, tq=tq, tk=tk, seq_len=seq_len)
    return pl.pallas_call(
        kern,
        out_shape=jax.ShapeDtypeStruct((nb, seq_len, n_q_cols * HEAD_DIM), BF16),
        grid=(nb, n_kv_heads),
        in_specs=[
            pl.BlockSpec(memory_space=pltpu.SMEM),
            pl.BlockSpec((None, seq_len, GROUP * HEAD_DIM), lambda b, h: (b, 0, h)),
            pl.BlockSpec((None, seq_len, HEAD_DIM), lambda b, h: (b, 0, n_q_cols + h)),
            pl.BlockSpec((None, seq_len, HEAD_DIM), lambda b, h: (b, 0, n_q_cols + n_kv_heads + h)),
        ],
        out_specs=pl.BlockSpec((None, seq_len, GROUP * HEAD_DIM), lambda b, h: (b, 0, h)),
        compiler_params=_params(("arbitrary", "arbitrary")),
        name="window_sink_attention",
    )(sink.reshape(1, -1).astype(F32), qkv, qkv, qkv)


SUPER = 1024


def _dilate_kernel(x_ref, r4_ref, r16_ref, slab):
    for c in range(x_ref.shape[1] // LANES):
        cols = slice(c * LANES, (c + 1) * LANES)
        slab[...] = x_ref[:, cols].astype(F32)
        for res in range(4):
            r4_ref[res, :, cols] = slab[pl.ds(res, SUPER // 4, stride=4), :].astype(r4_ref.dtype)
        for res in range(16):
            r16_ref[res, :, cols] = slab[pl.ds(res, SUPER // 16, stride=16), :].astype(r16_ref.dtype)


def dilate(qkv, bw=512):
    nb, s, c = qkv.shape
    return pl.pallas_call(
        _dilate_kernel,
        out_shape=(jax.ShapeDtypeStruct((nb, 4, s // 4, c), BF16),
                   jax.ShapeDtypeStruct((nb, 16, s // 16, c), BF16)),
        grid=(nb, s // SUPER, c // bw),
        in_specs=[pl.BlockSpec((None, SUPER, bw), lambda b, i, j: (b, i, j))],
        out_specs=(pl.BlockSpec((None, 4, SUPER // 4, bw), lambda b, i, j: (b, 0, i, j)),
                   pl.BlockSpec((None, 16, SUPER // 16, bw), lambda b, i, j: (b, 0, i, j))),
        scratch_shapes=[pltpu.VMEM((SUPER, LANES), F32)],
        compiler_params=_params(("parallel", "parallel", "parallel")),
        name="dilate",
    )(qkv)


DIL_HALF = 64
DIL_TQ = 256


def _softmax_t(st, valid):
    s = jnp.where(valid, st, NEG_INF)
    m = jnp.max(s, axis=0, keepdims=True)
    p = jnp.exp2(s - m)
    den = jnp.sum(p, axis=0, keepdims=True)
    lse = (m + jnp.log2(den)) * LN2
    return p.astype(BF16), den, lse


def _dilated_kernel(q1_ref, k1_ref, v1_ref, q4_ref, k4_ref, v4_ref, q16_ref, k16_ref, v16_ref,
                    out_ref, o_scr, st1, st4, st16, *, seq_len):
    n = pl.program_id(2)
    tq, w = DIL_TQ, DIL_HALF
    nq = GROUP * tq

    def stat_tile(lse):
        rows = [lse[:, g * tq:(g + 1) * tq] for g in range(GROUP)]
        rows.append(jnp.zeros((LANES - GROUP, tq), F32))
        return jnp.concatenate(rows, axis=0).T

    def banded_tile(q, k_ref_sl, v_ref_sl, q0, seq, tk):
        start = pl.multiple_of(jnp.clip(q0 - w, 0, seq - tk), 16)
        k = k_ref_sl(start, tk)
        v = v_ref_sl(start, tk)
        rel = (lax.broadcasted_iota(jnp.int32, (tk, tq), 1)
               - lax.broadcasted_iota(jnp.int32, (tk, tq), 0))
        dist = rel + (q0 - start)
        valid = (dist >= -w) & (dist <= w)
        st = _scores_t(k, _head_rows(q))
        parts = [_softmax_t(st[:, g * tq:(g + 1) * tq], valid) for g in range(GROUP)]
        p, den, lse = (jnp.concatenate([part[i] for part in parts], axis=1) for i in range(3))
        ot = jnp.dot(v.T, p, preferred_element_type=F32) / den
        return ot, lse

    tk1 = min(tq + 2 * w, seq_len)

    def branch1(j, carry):
        r0 = pl.multiple_of(j * tq, tq)
        ot, lse = banded_tile(q1_ref[pl.ds(r0, tq), :],
                              lambda s0, tk: k1_ref[pl.ds(s0, tk), :],
                              lambda s0, tk: v1_ref[pl.ds(s0, tk), :],
                              n * SUPER + r0, seq_len, tk1)
        for g in range(GROUP):
            o_scr[g, pl.ds(r0, tq), :] = ot[:, g * tq:(g + 1) * tq].T
        st1[pl.ds(r0, tq), :] = stat_tile(lse)
        return carry

    lax.fori_loop(0, SUPER // tq, branch1, 0)

    len4 = seq_len // 4
    tk4 = min(tq + 2 * w, len4)
    assert SUPER // 4 == tq

    def branch4(cls, carry):
        ot, lse = banded_tile(q4_ref[cls],
                              lambda s0, tk: k4_ref[cls, pl.ds(s0, tk), :],
                              lambda s0, tk: v4_ref[cls, pl.ds(s0, tk), :],
                              n * tq, len4, tk4)
        for g in range(GROUP):
            o_scr[GROUP + g, pl.ds(cls, tq, stride=4), :] = ot[:, g * tq:(g + 1) * tq].T
        st4[pl.ds(cls, tq, stride=4), :] = stat_tile(lse)
        return carry

    lax.fori_loop(0, 4, branch4, 0)

    len16 = seq_len // 16
    nl = SUPER // 16
    ncls = 4
    lane16 = lax.broadcasted_iota(jnp.int32, (len16, ncls * GROUP * nl), 1)
    row16 = lax.broadcasted_iota(jnp.int32, (len16, ncls * GROUP * nl), 0)
    dist16 = (lane16 & (nl - 1)) - row16 + n * nl
    valid16 = (dist16 >= -w) & (dist16 <= w)

    def branch16(cg, carry):
        cids = [cg * ncls + cc for cc in range(ncls)]
        st = jnp.concatenate([_scores_t(k16_ref[c], _head_rows(q16_ref[c])) for c in cids], axis=1)
        p, den, lse = _softmax_t(st, valid16)
        cw = GROUP * nl
        for cc, c in enumerate(cids):
            sl = slice(cc * cw, (cc + 1) * cw)
            ot = jnp.dot(v16_ref[c].T, p[:, sl], preferred_element_type=F32) / den[:, sl]
            rows = ot.T
            stat = jnp.broadcast_to(lse[:, sl], (LANES, cw)).T
            for g in range(GROUP):
                o_scr[2 * GROUP + g, pl.ds(c, nl, stride=16), :] = rows[g * nl:(g + 1) * nl]
                st16[g, pl.ds(c, nl, stride=16), :] = stat[g * nl:(g + 1) * nl]
        return carry

    lax.fori_loop(0, 16 // ncls, branch16, 0)

    for g in range(GROUP):
        a1 = st1[:, g:g + 1]
        a4 = st4[:, g:g + 1]
        a16 = st16[g]
        mx = jnp.maximum(jnp.maximum(a1, a4), a16)
        w1 = jnp.exp(a1 - mx)
        w4 = jnp.exp(a4 - mx)
        w16 = jnp.exp(a16 - mx)
        o = (w1 * o_scr[g] + w4 * o_scr[GROUP + g] + w16 * o_scr[2 * GROUP + g]) / (w1 + w4 + w16)
        out_ref[:, g * HEAD_DIM:(g + 1) * HEAD_DIM] = o.astype(out_ref.dtype)


def dilated_attention(nat, r4, r16, n_kv_heads):
    nb, s, c = nat.shape
    gw = GROUP * HEAD_DIM
    nqc = n_kv_heads * GROUP
    kcol = lambda h: nqc + h
    vcol = lambda h: nqc + n_kv_heads + h
    in_specs = [
        pl.BlockSpec((None, SUPER, gw), lambda b, h, n: (b, n, h)),
        pl.BlockSpec((None, s, HEAD_DIM), lambda b, h, n: (b, 0, kcol(h))),
        pl.BlockSpec((None, s, HEAD_DIM), lambda b, h, n: (b, 0, vcol(h))),
        pl.BlockSpec((None, 4, SUPER // 4, gw), lambda b, h, n: (b, 0, n, h)),
        pl.BlockSpec((None, 4, s // 4, HEAD_DIM), lambda b, h, n: (b, 0, 0, kcol(h))),
        pl.BlockSpec((None, 4, s // 4, HEAD_DIM), lambda b, h, n: (b, 0, 0, vcol(h))),
        pl.BlockSpec((None, 16, SUPER // 16, gw), lambda b, h, n: (b, 0, n, h)),
        pl.BlockSpec((None, 16, s // 16, HEAD_DIM), lambda b, h, n: (b, 0, 0, kcol(h))),
        pl.BlockSpec((None, 16, s // 16, HEAD_DIM), lambda b, h, n: (b, 0, 0, vcol(h))),
    ]
    return pl.pallas_call(
        functools.partial(_dilated_kernel, seq_len=s),
        out_shape=jax.ShapeDtypeStruct((nb, s, nqc * HEAD_DIM), BF16),
        grid=(nb, n_kv_heads, s // SUPER),
        in_specs=in_specs,
        out_specs=pl.BlockSpec((None, SUPER, gw), lambda b, h, n: (b, n, h)),
        scratch_shapes=[pltpu.VMEM((3 * GROUP, SUPER, LANES), F32),
                        pltpu.VMEM((SUPER, LANES), F32),
                        pltpu.VMEM((SUPER, LANES), F32),
                        pltpu.VMEM((GROUP, SUPER, LANES), F32)],
        compiler_params=_params(("arbitrary", "arbitrary", "arbitrary")),
        name="dilated_attention",
    )(nat, nat, nat, r4, r4, r4, r16, r16, r16)


def kernel(x, c, positions, w_qkv, w_o, sink, w_gate_up, w_down, g_mix_pre, g_mix_post,
           g_ffn_pre, g_ffn_post, w_ada, b_ada):
    nb, s, d = x.shape
    depth = w_qkv.shape[0]
    t = nb * s
    q_dim = w_o.shape[1]
    kv_dim = (w_qkv.shape[2] - q_dim) // 2
    n_kv_heads = kv_dim // HEAD_DIM
    d_ff = w_down.shape[1]
    assert [r for _, r in DILATED_BRANCHES] == [1, 4, 16] and all(w // (2 * r) == 64 for w, r in DILATED_BRANCHES)

    c_pad = jnp.pad(c, ((0, 8 - nb), (0, 0)))
    mod = ada_modulation(c_pad, w_ada, b_ada)
    mod4 = mod.reshape(depth, 8, 1, 6 * d)
    cos, sin = rope_tables(positions.reshape(t, 1))

    g_mix_pre3 = g_mix_pre.reshape(depth, 1, d)
    g_mix_post3 = g_mix_post.reshape(depth, 1, d)
    g_ffn_pre3 = g_ffn_pre.reshape(depth, 1, d)
    g_ffn_post3 = g_ffn_post.reshape(depth, 1, d)
    SH_T, SC_T, GT_T, SH_F, SC_F, GT_F = range(6)

    h = prenorm(x, g_mix_pre3, mod4, 0, SC_T, SH_T)
    for i in range(depth):
        h2 = h.reshape(t, d)
        qkv = qkv_projection(h2, w_qkv, i, cos, sin, q_dim, kv_dim).reshape(nb, s, -1)
        if i % N_MIXERS == 0:
            att = window_sink_attention(qkv, WINDOW_A_HALF, n_kv_heads, sink[i // N_MIXERS])
        else:
            r4, r16 = dilate(qkv)
            att = dilated_attention(qkv, r4, r16, n_kv_heads)
        y = matmul_f32w(att.reshape(t, q_dim), w_o, i, bm=2048, bn=512, name="wo_projection")
        x, h = post_residual(x, y.reshape(nb, s, d), g_mix_post3, mod4, i, GT_T,
                             nxt=(g_ffn_pre3, i, SC_F, SH_F))
        a, wd_b = gate_up(h.reshape(t, d), w_gate_up, w_down, i, d_ff)
        y = matmul_bf16w(a, wd_b, bm=512, bn=512, name="down_projection")
        nxt = (g_mix_pre3, i + 1, SC_T, SH_T) if i + 1 < depth else None
        x, h = post_residual(x, y.reshape(nb, s, d), g_ffn_post3, mod4, i, GT_F, nxt=nxt)
    return x
```

```python
import functools
import math

import jax
import jax.numpy as jnp
from jax import lax
from jax.experimental import pallas as pl
from jax.experimental.pallas import tpu as pltpu

HEAD_DIM = 128
GROUP = 4
ROPE_THETA = 10000.0
WINDOW_A_HALF = 128
DILATED_BRANCHES = ((128, 1), (512, 4), (2048, 16))
RMS_EPS = 1e-6
NEG_INF = -1e30
N_MIXERS = 2
LOG2E = 1.4426950408889634
LN2 = 0.6931471805599453

LANES = 128
VMEM_LIMIT = 58 * 1024 * 1024

F32 = jnp.float32
BF16 = jnp.bfloat16


def _params(sem, vmem=VMEM_LIMIT):
    return pltpu.CompilerParams(dimension_semantics=sem, vmem_limit_bytes=vmem)


def _ada_kernel(c_ref, w_ref, b_ref, o_ref):
    c = c_ref[...].astype(BF16)
    w = w_ref[...].astype(BF16)
    o_ref[...] = jnp.dot(c, w, preferred_element_type=F32) + b_ref[...]


def ada_modulation(c_pad, w_ada, b_ada, bn=512):
    nl, d, n = w_ada.shape
    rows = c_pad.shape[0]
    return pl.pallas_call(
        _ada_kernel,
        out_shape=jax.ShapeDtypeStruct((nl, rows, n), F32),
        grid=(nl, n // bn),
        in_specs=[
            pl.BlockSpec((rows, d), lambda l, j: (0, 0)),
            pl.BlockSpec((None, d, bn), lambda l, j: (l, 0, j)),
            pl.BlockSpec((None, 1, bn), lambda l, j: (l, 0, j)),
        ],
        out_specs=pl.BlockSpec((None, rows, bn), lambda l, j: (l, 0, j)),
        compiler_params=_params(("parallel", "parallel")),
        name="ada_modulation",
    )(c_pad, w_ada, b_ada.reshape(nl, 1, n))


def _rope_table_kernel(pos_ref, cos_ref, sin_ref):
    half = HEAD_DIM // 2
    pos = pos_ref[...].astype(F32)
    lane = lax.broadcasted_iota(jnp.int32, (1, HEAD_DIM), 1)
    fidx = (lane & (half - 1)).astype(F32)
    inv_freq = jnp.exp(-math.log(ROPE_THETA) * fidx / half)
    ang = pos * inv_freq
    cos_ref[...] = jnp.cos(ang)
    s = jnp.sin(ang)
    sin_ref[...] = jnp.where(lane < half, -s, s)


def rope_tables(pos_col, bm=1024):
    t = pos_col.shape[0]
    out = jax.ShapeDtypeStruct((t, HEAD_DIM), F32)
    return pl.pallas_call(
        _rope_table_kernel,
        out_shape=(out, out),
        grid=(t // bm,),
        in_specs=[pl.BlockSpec((bm, 1), lambda i: (i, 0))],
        out_specs=(pl.BlockSpec((bm, HEAD_DIM), lambda i: (i, 0)),
                   pl.BlockSpec((bm, HEAD_DIM), lambda i: (i, 0))),
        compiler_params=_params(("parallel",)),
        name="rope_tables",
    )(pos_col)


def _rms(x, g):
    ms = jnp.mean(x * x, axis=-1, keepdims=True)
    return x * lax.rsqrt(ms + RMS_EPS) * g


def _prenorm_kernel(x_ref, g_ref, sc_ref, sh_ref, h_ref):
    y = _rms(x_ref[...], g_ref[...])
    h_ref[...] = (y * (1.0 + sc_ref[...]) + sh_ref[...]).astype(h_ref.dtype)


def _mod_spec(layer, chunk, d):
    return pl.BlockSpec((None, None, 1, d), lambda b, i: (layer, b, 0, chunk))


def _vec_spec(layer, d):
    return pl.BlockSpec((None, 1, d), lambda b, i: (layer, 0, 0))


def prenorm(x, g3, mod4, layer, sc_chunk, sh_chunk, bm=256):
    nb, s, d = x.shape
    row = pl.BlockSpec((None, bm, d), lambda b, i: (b, i, 0))
    return pl.pallas_call(
        _prenorm_kernel,
        out_shape=jax.ShapeDtypeStruct(x.shape, BF16),
        grid=(nb, s // bm),
        in_specs=[row, _vec_spec(layer, d), _mod_spec(layer, sc_chunk, d), _mod_spec(layer, sh_chunk, d)],
        out_specs=row,
        compiler_params=_params(("parallel", "parallel")),
        name="prenorm",
    )(x, g3, mod4, mod4)


def _post_kernel(x_ref, y_ref, gpost_ref, gate_ref, xo_ref):
    yn = _rms(y_ref[...].astype(F32), gpost_ref[...])
    xo_ref[...] = x_ref[...] + gate_ref[...] * yn


def _post_next_kernel(x_ref, y_ref, gpost_ref, gate_ref, gpre_ref, sc_ref, sh_ref, xo_ref, h_ref):
    yn = _rms(y_ref[...].astype(F32), gpost_ref[...])
    xn = x_ref[...] + gate_ref[...] * yn
    xo_ref[...] = xn
    hn = _rms(xn, gpre_ref[...])
    h_ref[...] = (hn * (1.0 + sc_ref[...]) + sh_ref[...]).astype(h_ref.dtype)


def post_residual(x, y, gpost3, mod4, layer, gate_chunk, nxt=None, bm=256):
    nb, s, d = x.shape
    row = pl.BlockSpec((None, bm, d), lambda b, i: (b, i, 0))
    in_specs = [row, row, _vec_spec(layer, d), _mod_spec(layer, gate_chunk, d)]
    args = [x, y, gpost3, mod4]
    if nxt is None:
        return pl.pallas_call(
            _post_kernel,
            out_shape=jax.ShapeDtypeStruct(x.shape, F32),
            grid=(nb, s // bm),
            in_specs=in_specs,
            out_specs=row,
            compiler_params=_params(("parallel", "parallel")),
            name="post_residual",
        )(*args), None
    g3n, ln, scn, shn = nxt
    in_specs += [_vec_spec(ln, d), _mod_spec(ln, scn, d), _mod_spec(ln, shn, d)]
    args += [g3n, mod4, mod4]
    return pl.pallas_call(
        _post_next_kernel,
        out_shape=(jax.ShapeDtypeStruct(x.shape, F32), jax.ShapeDtypeStruct(x.shape, BF16)),
        grid=(nb, s // bm),
        in_specs=in_specs,
        out_specs=(row, row),
        compiler_params=_params(("parallel", "parallel")),
        name="post_residual_next",
    )(*args)


def _act_spec(bm, k):
    return pl.BlockSpec((bm, k), lambda i, j: (i, 0), pipeline_mode=pl.Buffered(1))


def _mm_kernel(a_ref, w_ref, o_ref):
    w = w_ref[...].astype(BF16)
    o_ref[...] = jnp.dot(a_ref[...], w, preferred_element_type=F32).astype(o_ref.dtype)


def matmul_f32w(a, w3, layer, bm, bn, out_dtype=BF16, name="matmul"):
    m, k = a.shape
    n = w3.shape[2]
    return pl.pallas_call(
        _mm_kernel,
        out_shape=jax.ShapeDtypeStruct((m, n), out_dtype),
        grid=(m // bm, n // bn),
        in_specs=[_act_spec(bm, k),
                  pl.BlockSpec((None, k, bn), lambda i, j: (layer, 0, j))],
        out_specs=pl.BlockSpec((bm, bn), lambda i, j: (i, j)),
        compiler_params=_params(("arbitrary", "arbitrary")),
        name=name,
    )(a, w3)


def matmul_bf16w(a, w2, bm, bn, out_dtype=BF16, name="matmul"):
    m, k = a.shape
    n = w2.shape[1]
    return pl.pallas_call(
        _mm_kernel,
        out_shape=jax.ShapeDtypeStruct((m, n), out_dtype),
        grid=(m // bm, n // bn),
        in_specs=[pl.BlockSpec((bm, k), lambda i, j: (i, 0)),
                  pl.BlockSpec((k, bn), lambda i, j: (0, j))],
        out_specs=pl.BlockSpec((bm, bn), lambda i, j: (i, j)),
        compiler_params=_params(("arbitrary", "arbitrary")),
        name=name,
    )(a, w2)


def _qkv_kernel(a_ref, w_ref, cos_ref, sin_ref, o_ref, *, n_q_blocks, n_rope_blocks, q_scale):
    j = pl.program_id(1)
    w = w_ref[...].astype(BF16)
    is_rope = j < n_rope_blocks
    mult = jnp.where(j < n_q_blocks, q_scale, 1.0).astype(F32)
    n_heads = w.shape[1] // HEAD_DIM
    part = a_ref.shape[0] // 2
    for r in range(2):
        rows = slice(r * part, (r + 1) * part)
        acc = jnp.dot(a_ref[rows, :], w, preferred_element_type=F32)
        cos = jnp.where(is_rope, cos_ref[rows, :] * mult, 1.0)
        sin = jnp.where(is_rope, sin_ref[rows, :] * mult, 0.0)
        for h in range(n_heads):
            t = acc[:, h * HEAD_DIM:(h + 1) * HEAD_DIM]
            rot = pltpu.roll(t, HEAD_DIM // 2, 1)
            o_ref[rows, h * HEAD_DIM:(h + 1) * HEAD_DIM] = (t * cos + rot * sin).astype(o_ref.dtype)


def qkv_projection(h2, w3, layer, cos, sin, q_dim, kv_dim, bm=2048, bn=512):
    m, k = h2.shape
    n = w3.shape[2]
    kern = functools.partial(_qkv_kernel, n_q_blocks=q_dim // bn, n_rope_blocks=(q_dim + kv_dim) // bn,
                             q_scale=LOG2E / math.sqrt(HEAD_DIM))
    tab = pl.BlockSpec((bm, HEAD_DIM), lambda i, j: (i, 0))
    return pl.pallas_call(
        kern,
        out_shape=jax.ShapeDtypeStruct((m, n), BF16),
        grid=(m // bm, n // bn),
        in_specs=[_act_spec(bm, k),
                  pl.BlockSpec((None, k, bn), lambda i, j: (layer, 0, j)),
                  tab, tab],
        out_specs=pl.BlockSpec((bm, bn), lambda i, j: (i, j)),
        compiler_params=_params(("arbitrary", "arbitrary")),
        name="qkv_projection",
    )(h2, w3, cos, sin)


def _gate_up_kernel(a_ref, wg_ref, wu_ref, wd_ref, o_ref, wdo_ref):
    a = a_ref[...]
    g = jnp.dot(a, wg_ref[...].astype(BF16), preferred_element_type=F32)
    u = jnp.dot(a, wu_ref[...].astype(BF16), preferred_element_type=F32)
    o_ref[...] = (g * jax.nn.sigmoid(g) * u).astype(o_ref.dtype)
    wdo_ref[...] = wd_ref[...].astype(wdo_ref.dtype)


def gate_up(h2, w3, w_down3, layer, d_ff, bm=2048, bn=256):
    m, k = h2.shape
    d_out = w_down3.shape[2]
    nblk = d_ff // bn
    steps = (m // bm) * nblk
    slab = d_ff // steps
    assert slab * steps == d_ff and slab % 16 == 0
    return pl.pallas_call(
        _gate_up_kernel,
        out_shape=(jax.ShapeDtypeStruct((m, d_ff), BF16), jax.ShapeDtypeStruct((d_ff, d_out), BF16)),
        grid=(m // bm, nblk),
        in_specs=[_act_spec(bm, k),
                  pl.BlockSpec((None, k, bn), lambda i, j: (layer, 0, j)),
                  pl.BlockSpec((None, k, bn), lambda i, j: (layer, 0, j + nblk)),
                  pl.BlockSpec((None, slab, d_out), lambda i, j: (layer, i * nblk + j, 0))],
        out_specs=(pl.BlockSpec((bm, bn), lambda i, j: (i, j)),
                   pl.BlockSpec((slab, d_out), lambda i, j: (i * nblk + j, 0))),
        compiler_params=_params(("arbitrary", "arbitrary")),
        name="gate_up",
    )(h2, w3, w3, w_down3)


def _scores_t(k, qs):
    return lax.dot_general(k, qs, (((1,), (1,)), ((), ())), preferred_element_type=F32)


def _head_rows(q):
    return jnp.concatenate([q[:, g * HEAD_DIM:(g + 1) * HEAD_DIM] for g in range(GROUP)], axis=0)


ATT_TQ = LANES
ATT_UNROLL = 4


def _fill_band_masks(mask_ref, half_w):
    _, tk, tq = mask_ref.shape
    rel = (lax.broadcasted_iota(jnp.int32, (tk, tq), 1)
           - lax.broadcasted_iota(jnp.int32, (tk, tq), 0))
    for v in range(3):
        dist = rel + v * half_w
        mask_ref[v] = ((dist >= -half_w) & (dist <= half_w)).astype(jnp.int32)


def _window_start(q0, half_w, tk, seq_len):
    start = pl.multiple_of(jnp.clip(q0 - half_w, 0, seq_len - tk), 16)
    return start, (q0 - start) // half_w


def _tile_scores(q, k, st_buf):
    st_buf[...] = _scores_t(k, _head_rows(q))


def _tile_finish(v, mask, st_buf, p_buf, sinks=None):
    tq = mask.shape[1]
    keep = mask != 0
    dens, lses = [], []
    for g in range(GROUP):
        cols = slice(g * tq, (g + 1) * tq)
        s = jnp.where(keep, st_buf[:, cols], NEG_INF)
        m = jnp.max(s, axis=0, keepdims=True)
        if sinks is not None:
            m = jnp.maximum(m, sinks[g])
        p = jnp.exp2(s - m)
        den = jnp.sum(p, axis=0, keepdims=True)
        if sinks is not None:
            den = den + jnp.exp2(sinks[g] - m)
        p_buf[:, cols] = p.astype(BF16)
        dens.append(den)
        lses.append(m + jnp.log2(den))
    den = jnp.concatenate(dens, axis=1)
    ot = jnp.dot(v.T, p_buf[...], preferred_element_type=F32) / den
    return ot, jnp.concatenate(lses, axis=1)


def _run_tiles(n_tiles, scores, finish):
    assert n_tiles % ATT_UNROLL == 0
    scores(0, 0)

    def body(it, carry):
        for u in range(ATT_UNROLL):
            t = it * ATT_UNROLL + u
            scores(jnp.minimum(t + 1, n_tiles - 1), (u + 1) % ATT_UNROLL)
            finish(t, u)
        return carry

    lax.fori_loop(0, n_tiles // ATT_UNROLL, body, 0)


def _window_sink_kernel(sink_ref, q_ref, k_ref, v_ref, o_ref, mask_ref, st_buf, p_buf,
                        *, half_w, tk, seq_len):
    h = pl.program_id(1)
    tq = ATT_TQ
    _fill_band_masks(mask_ref, half_w)
    sinks = [sink_ref[0, h * GROUP + g] * LOG2E for g in range(GROUP)]

    def scores(t, slot):
        q0 = pl.multiple_of(t * tq, tq)
        start, _ = _window_start(q0, half_w, tk, seq_len)
        _tile_scores(q_ref[pl.ds(q0, tq), :], k_ref[pl.ds(start, tk), :], st_buf.at[slot])

    def finish(t, slot):
        q0 = pl.multiple_of(t * tq, tq)
        start, variant = _window_start(q0, half_w, tk, seq_len)
        ot, _ = _tile_finish(v_ref[pl.ds(start, tk), :], mask_ref[variant],
                             st_buf.at[slot], p_buf.at[slot], sinks)
        for g in range(GROUP):
            o_ref[pl.ds(q0, tq), g * HEAD_DIM:(g + 1) * HEAD_DIM] = (
                ot[:, g * tq:(g + 1) * tq].T.astype(o_ref.dtype))

    _run_tiles(seq_len // tq, scores, finish)


def window_sink_attention(qkv, half_w, n_kv_heads, sink):
    nb, seq_len, _ = qkv.shape
    tq = ATT_TQ
    tk = tq + 2 * half_w
    assert tq % half_w == 0 and (seq_len - tk) % half_w == 0 and seq_len % (tq * ATT_UNROLL) == 0
    n_q_cols = n_kv_heads * GROUP
    kern = functools.partial(_window_sink_kernel, half_w=half_w, tk=tk, seq_len=seq_len)
    return pl.pallas_call(
        kern,
        out_shape=jax.ShapeDtypeStruct((nb, seq_len, n_q_cols * HEAD_DIM), BF16),
        grid=(nb, n_kv_heads),
        in_specs=[
            pl.BlockSpec(memory_space=pltpu.SMEM),
            pl.BlockSpec((None, seq_len, GROUP * HEAD_DIM), lambda b, h: (b, 0, h)),
            pl.BlockSpec((None, seq_len, HEAD_DIM), lambda b, h: (b, 0, n_q_cols + h)),
            pl.BlockSpec((None, seq_len, HEAD_DIM), lambda b, h: (b, 0, n_q_cols + n_kv_heads + h)),
        ],
        out_specs=pl.BlockSpec((None, seq_len, GROUP * HEAD_DIM), lambda b, h: (b, 0, h)),
        scratch_shapes=[pltpu.VMEM((3, tk, tq), jnp.int32),
                        pltpu.VMEM((ATT_UNROLL, tk, GROUP * tq), F32),
                        pltpu.VMEM((ATT_UNROLL, tk, GROUP * tq), BF16)],
        compiler_params=_params(("arbitrary", "arbitrary")),
        name="window_sink_attention",
    )(sink.reshape(1, -1).astype(F32), qkv, qkv, qkv)


SUPER = 1024


def _dilate_kernel(x_ref, r4_ref, r16_ref, slab):
    for c in range(x_ref.shape[1] // LANES):
        cols = slice(c * LANES, (c + 1) * LANES)
        slab[...] = x_ref[:, cols].astype(F32)
        for res in range(4):
            r4_ref[res, :, cols] = slab[pl.ds(res, SUPER // 4, stride=4), :].astype(r4_ref.dtype)
        for res in range(16):
            r16_ref[res, :, cols] = slab[pl.ds(res, SUPER // 16, stride=16), :].astype(r16_ref.dtype)


def dilate(qkv, bw=512):
    nb, s, c = qkv.shape
    return pl.pallas_call(
        _dilate_kernel,
        out_shape=(jax.ShapeDtypeStruct((nb, 4, s // 4, c), BF16),
                   jax.ShapeDtypeStruct((nb, 16, s // 16, c), BF16)),
        grid=(nb, s // SUPER, c // bw),
        in_specs=[pl.BlockSpec((None, SUPER, bw), lambda b, i, j: (b, i, j))],
        out_specs=(pl.BlockSpec((None, 4, SUPER // 4, bw), lambda b, i, j: (b, 0, i, j)),
                   pl.BlockSpec((None, 16, SUPER // 16, bw), lambda b, i, j: (b, 0, i, j))),
        scratch_shapes=[pltpu.VMEM((SUPER, LANES), F32)],
        compiler_params=_params(("parallel", "parallel", "parallel")),
        name="dilate",
    )(qkv)


DIL_HALF = 64
DIL_SUPER = 2048


def _dilated_kernel(q1_ref, k1_ref, v1_ref, q4_ref, k4_ref, v4_ref, q16_ref, k16_ref, v16_ref,
                    out_ref, mask_ref, st_buf, p_buf, o_scr, st1, st4, st16, *, seq_len):
    n = pl.program_id(2)
    tq, w = ATT_TQ, DIL_HALF
    tk = tq + 2 * w
    n_tiles = DIL_SUPER // tq
    _fill_band_masks(mask_ref, w)

    def stat_tile(lse2):
        rows = [lse2[:, g * tq:(g + 1) * tq] * LN2 for g in range(GROUP)]
        rows.append(jnp.zeros((LANES - GROUP, tq), F32))
        return jnp.concatenate(rows, axis=0).T

    def branch(seq, geometry, load_q, load_kv, out_rows, base, stat_ref):
        def scores(t, slot):
            q0, hnd = geometry(t)
            start, _ = _window_start(q0, w, tk, seq)
            _tile_scores(load_q(hnd), load_kv(0, hnd, start), st_buf.at[slot])

        def finish(t, slot):
            q0, hnd = geometry(t)
            start, variant = _window_start(q0, w, tk, seq)
            ot, lse2 = _tile_finish(load_kv(1, hnd, start), mask_ref[variant],
                                    st_buf.at[slot], p_buf.at[slot])
            rows = out_rows(hnd)
            for g in range(GROUP):
                o_scr[base + g, rows, :] = ot[:, g * tq:(g + 1) * tq].T
            stat_ref[rows, :] = stat_tile(lse2)

        _run_tiles(n_tiles, scores, finish)

    kv1 = (k1_ref, v1_ref)
    branch(seq_len,
           lambda t: (n * DIL_SUPER + t * tq, pl.multiple_of(t * tq, tq)),
           lambda r0: q1_ref[pl.ds(r0, tq), :],
           lambda i, r0, start: kv1[i][pl.ds(start, tk), :],
           lambda r0: pl.ds(r0, tq),
           0, st1)

    per4 = DIL_SUPER // 4
    kv4 = (k4_ref, v4_ref)
    branch(seq_len // 4,
           lambda t: (n * per4 + (t & 3) * tq, (t >> 2, pl.multiple_of((t & 3) * tq, tq))),
           lambda h: q4_ref[h[0], pl.ds(h[1], tq), :],
           lambda i, h, start: kv4[i][h[0], pl.ds(start, tk), :],
           lambda h: pl.ds(4 * h[1] + h[0], tq, stride=4),
           GROUP, st4)

    assert DIL_SUPER // 16 == tq
    kv16 = (k16_ref, v16_ref)
    branch(seq_len // 16,
           lambda t: (n * tq, t),
           lambda c: q16_ref[c],
           lambda i, c, start: kv16[i][c, pl.ds(start, tk), :],
           lambda c: pl.ds(c, tq, stride=16),
           2 * GROUP, st16)

    a1, a4, a16 = st1[...], st4[...], st16[...]
    mx = jnp.maximum(jnp.maximum(a1, a4), a16)
    e1 = jnp.exp(a1 - mx)
    e4 = jnp.exp(a4 - mx)
    inv = 1.0 / (e1 + e4 + jnp.exp(a16 - mx))
    alpha1 = e1 * inv
    alpha4 = e4 * inv
    for g in range(GROUP):
        o16 = o_scr[2 * GROUP + g]
        o = (o16 + alpha1[:, g:g + 1] * (o_scr[g] - o16)
             + alpha4[:, g:g + 1] * (o_scr[GROUP + g] - o16))
        out_ref[:, g * HEAD_DIM:(g + 1) * HEAD_DIM] = o.astype(out_ref.dtype)


def dilated_attention(nat, r4, r16, n_kv_heads):
    nb, s, c = nat.shape
    gw = GROUP * HEAD_DIM
    nqc = n_kv_heads * GROUP
    kcol = lambda h: nqc + h
    vcol = lambda h: nqc + n_kv_heads + h
    sup, tq = DIL_SUPER, ATT_TQ
    tk = tq + 2 * DIL_HALF
    assert s % sup == 0 and s // 16 >= tk and tq % DIL_HALF == 0
    in_specs = [
        pl.BlockSpec((None, sup, gw), lambda b, h, n: (b, n, h)),
        pl.BlockSpec((None, s, HEAD_DIM), lambda b, h, n: (b, 0, kcol(h))),
        pl.BlockSpec((None, s, HEAD_DIM), lambda b, h, n: (b, 0, vcol(h))),
        pl.BlockSpec((None, 4, sup // 4, gw), lambda b, h, n: (b, 0, n, h)),
        pl.BlockSpec((None, 4, s // 4, HEAD_DIM), lambda b, h, n: (b, 0, 0, kcol(h))),
        pl.BlockSpec((None, 4, s // 4, HEAD_DIM), lambda b, h, n: (b, 0, 0, vcol(h))),
        pl.BlockSpec((None, 16, sup // 16, gw), lambda b, h, n: (b, 0, n, h)),
        pl.BlockSpec((None, 16, s // 16, HEAD_DIM), lambda b, h, n: (b, 0, 0, kcol(h))),
        pl.BlockSpec((None, 16, s // 16, HEAD_DIM), lambda b, h, n: (b, 0, 0, vcol(h))),
    ]
    return pl.pallas_call(
        functools.partial(_dilated_kernel, seq_len=s),
        out_shape=jax.ShapeDtypeStruct((nb, s, nqc * HEAD_DIM), BF16),
        grid=(nb, n_kv_heads, s // sup),
        in_specs=in_specs,
        out_specs=pl.BlockSpec((None, sup, gw), lambda b, h, n: (b, n, h)),
        scratch_shapes=[pltpu.VMEM((3, tk, tq), jnp.int32),
                        pltpu.VMEM((ATT_UNROLL, tk, GROUP * tq), F32),
                        pltpu.VMEM((ATT_UNROLL, tk, GROUP * tq), BF16),
                        pltpu.VMEM((3 * GROUP, sup, LANES), F32),
                        pltpu.VMEM((sup, LANES), F32),
                        pltpu.VMEM((sup, LANES), F32),
                        pltpu.VMEM((sup, LANES), F32)],
        compiler_params=_params(("arbitrary", "arbitrary", "arbitrary")),
        name="dilated_attention",
    )(nat, nat, nat, r4, r4, r4, r16, r16, r16)


def kernel(x, c, positions, w_qkv, w_o, sink, w_gate_up, w_down, g_mix_pre, g_mix_post,
           g_ffn_pre, g_ffn_post, w_ada, b_ada):
    nb, s, d = x.shape
    depth = w_qkv.shape[0]
    t = nb * s
    q_dim = w_o.shape[1]
    kv_dim = (w_qkv.shape[2] - q_dim) // 2
    n_kv_heads = kv_dim // HEAD_DIM
    d_ff = w_down.shape[1]
    assert [r for _, r in DILATED_BRANCHES] == [1, 4, 16] and all(w // (2 * r) == 64 for w, r in DILATED_BRANCHES)

    c_pad = jnp.pad(c, ((0, 8 - nb), (0, 0)))
    mod = ada_modulation(c_pad, w_ada, b_ada)
    mod4 = mod.reshape(depth, 8, 1, 6 * d)
    cos, sin = rope_tables(positions.reshape(t, 1))

    g_mix_pre3 = g_mix_pre.reshape(depth, 1, d)
    g_mix_post3 = g_mix_post.reshape(depth, 1, d)
    g_ffn_pre3 = g_ffn_pre.reshape(depth, 1, d)
    g_ffn_post3 = g_ffn_post.reshape(depth, 1, d)
    SH_T, SC_T, GT_T, SH_F, SC_F, GT_F = range(6)

    h = prenorm(x, g_mix_pre3, mod4, 0, SC_T, SH_T)
    for i in range(depth):
        h2 = h.reshape(t, d)
        qkv = qkv_projection(h2, w_qkv, i, cos, sin, q_dim, kv_dim).reshape(nb, s, -1)
        if i % N_MIXERS == 0:
            att = window_sink_attention(qkv, WINDOW_A_HALF, n_kv_heads, sink[i // N_MIXERS])
        else:
            r4, r16 = dilate(qkv)
            att = dilated_attention(qkv, r4, r16, n_kv_heads)
        y = matmul_f32w(att.reshape(t, q_dim), w_o, i, bm=2048, bn=512, name="wo_projection")
        x, h = post_residual(x, y.reshape(nb, s, d), g_mix_post3, mod4, i, GT_T,
                             nxt=(g_ffn_pre3, i, SC_F, SH_F))
        a, wd_b = gate_up(h.reshape(t, d), w_gate_up, w_down, i, d_ff)
        y = matmul_bf16w(a, wd_b, bm=512, bn=512, name="down_projection")
        nxt = (g_mix_pre3, i + 1, SC_T, SH_T) if i + 1 < depth else None
        x, h = post_residual(x, y.reshape(nb, s, d), g_ffn_post3, mod4, i, GT_F, nxt=nxt)
    return x
```

```python
import functools
import math

import jax
import jax.numpy as jnp
from jax import lax
from jax.experimental import pallas as pl
from jax.experimental.pallas import tpu as pltpu

HEAD_DIM = 128
GROUP = 4
ROPE_THETA = 10000.0
WINDOW_A_HALF = 128
DILATED_BRANCHES = ((128, 1), (512, 4), (2048, 16))
RMS_EPS = 1e-6
NEG_INF = -1e30
N_MIXERS = 2
LOG2E = 1.4426950408889634
LN2 = 0.6931471805599453

LANES = 128
VMEM_LIMIT = 58 * 1024 * 1024

F32 = jnp.float32
BF16 = jnp.bfloat16


def _params(sem, vmem=VMEM_LIMIT):
    return pltpu.CompilerParams(dimension_semantics=sem, vmem_limit_bytes=vmem)


def _ada_kernel(c_ref, w_ref, b_ref, o_ref):
    c = c_ref[...].astype(BF16)
    w = w_ref[...].astype(BF16)
    o_ref[...] = jnp.dot(c, w, preferred_element_type=F32) + b_ref[...]


def ada_modulation(c_pad, w_ada, b_ada, bn=512):
    nl, d, n = w_ada.shape
    rows = c_pad.shape[0]
    return pl.pallas_call(
        _ada_kernel,
        out_shape=jax.ShapeDtypeStruct((nl, rows, n), F32),
        grid=(nl, n // bn),
        in_specs=[
            pl.BlockSpec((rows, d), lambda l, j: (0, 0)),
            pl.BlockSpec((None, d, bn), lambda l, j: (l, 0, j)),
            pl.BlockSpec((None, 1, bn), lambda l, j: (l, 0, j)),
        ],
        out_specs=pl.BlockSpec((None, rows, bn), lambda l, j: (l, 0, j)),
        compiler_params=_params(("parallel", "parallel")),
        name="ada_modulation",
    )(c_pad, w_ada, b_ada.reshape(nl, 1, n))


def _rope_table_kernel(pos_ref, cos_ref, sin_ref):
    half = HEAD_DIM // 2
    pos = pos_ref[...].astype(F32)
    lane = lax.broadcasted_iota(jnp.int32, (1, HEAD_DIM), 1)
    fidx = (lane & (half - 1)).astype(F32)
    inv_freq = jnp.exp(-math.log(ROPE_THETA) * fidx / half)
    ang = pos * inv_freq
    cos_ref[...] = jnp.cos(ang)
    s = jnp.sin(ang)
    sin_ref[...] = jnp.where(lane < half, -s, s)


def rope_tables(pos_col, bm=1024):
    t = pos_col.shape[0]
    out = jax.ShapeDtypeStruct((t, HEAD_DIM), F32)
    return pl.pallas_call(
        _rope_table_kernel,
        out_shape=(out, out),
        grid=(t // bm,),
        in_specs=[pl.BlockSpec((bm, 1), lambda i: (i, 0))],
        out_specs=(pl.BlockSpec((bm, HEAD_DIM), lambda i: (i, 0)),
                   pl.BlockSpec((bm, HEAD_DIM), lambda i: (i, 0))),
        compiler_params=_params(("parallel",)),
        name="rope_tables",
    )(pos_col)


def _rms(x, g):
    ms = jnp.mean(x * x, axis=-1, keepdims=True)
    return x * lax.rsqrt(ms + RMS_EPS) * g


def _prenorm_kernel(x_ref, g_ref, sc_ref, sh_ref, h_ref):
    y = _rms(x_ref[...], g_ref[...])
    h_ref[...] = (y * (1.0 + sc_ref[...]) + sh_ref[...]).astype(h_ref.dtype)


def _mod_spec(layer, chunk, d):
    return pl.BlockSpec((None, None, 1, d), lambda b, i: (layer, b, 0, chunk))


def _vec_spec(layer, d):
    return pl.BlockSpec((None, 1, d), lambda b, i: (layer, 0, 0))


def prenorm(x, g3, mod4, layer, sc_chunk, sh_chunk, bm=256):
    nb, s, d = x.shape
    row = pl.BlockSpec((None, bm, d), lambda b, i: (b, i, 0))
    return pl.pallas_call(
        _prenorm_kernel,
        out_shape=jax.ShapeDtypeStruct(x.shape, BF16),
        grid=(nb, s // bm),
        in_specs=[row, _vec_spec(layer, d), _mod_spec(layer, sc_chunk, d), _mod_spec(layer, sh_chunk, d)],
        out_specs=row,
        compiler_params=_params(("parallel", "parallel")),
        name="prenorm",
    )(x, g3, mod4, mod4)


def _post_kernel(x_ref, y_ref, gpost_ref, gate_ref, xo_ref):
    yn = _rms(y_ref[...].astype(F32), gpost_ref[...])
    xo_ref[...] = x_ref[...] + gate_ref[...] * yn


def _post_next_kernel(x_ref, y_ref, gpost_ref, gate_ref, gpre_ref, sc_ref, sh_ref, xo_ref, h_ref):
    yn = _rms(y_ref[...].astype(F32), gpost_ref[...])
    xn = x_ref[...] + gate_ref[...] * yn
    xo_ref[...] = xn
    hn = _rms(xn, gpre_ref[...])
    h_ref[...] = (hn * (1.0 + sc_ref[...]) + sh_ref[...]).astype(h_ref.dtype)


def post_residual(x, y, gpost3, mod4, layer, gate_chunk, nxt=None, bm=256):
    nb, s, d = x.shape
    row = pl.BlockSpec((None, bm, d), lambda b, i: (b, i, 0))
    in_specs = [row, row, _vec_spec(layer, d), _mod_spec(layer, gate_chunk, d)]
    args = [x, y, gpost3, mod4]
    if nxt is None:
        return pl.pallas_call(
            _post_kernel,
            out_shape=jax.ShapeDtypeStruct(x.shape, F32),
            grid=(nb, s // bm),
            in_specs=in_specs,
            out_specs=row,
            compiler_params=_params(("parallel", "parallel")),
            name="post_residual",
        )(*args), None
    g3n, ln, scn, shn = nxt
    in_specs += [_vec_spec(ln, d), _mod_spec(ln, scn, d), _mod_spec(ln, shn, d)]
    args += [g3n, mod4, mod4]
    return pl.pallas_call(
        _post_next_kernel,
        out_shape=(jax.ShapeDtypeStruct(x.shape, F32), jax.ShapeDtypeStruct(x.shape, BF16)),
        grid=(nb, s // bm),
        in_specs=in_specs,
        out_specs=(row, row),
        compiler_params=_params(("parallel", "parallel")),
        name="post_residual_next",
    )(*args)


def _act_spec(bm, k):
    return pl.BlockSpec((bm, k), lambda i, j: (i, 0), pipeline_mode=pl.Buffered(1))


def _mm_kernel(a_ref, w_ref, o_ref):
    w = w_ref[...].astype(BF16)
    o_ref[...] = jnp.dot(a_ref[...], w, preferred_element_type=F32).astype(o_ref.dtype)


def matmul_f32w(a, w3, layer, bm, bn, out_dtype=BF16, name="matmul"):
    m, k = a.shape
    n = w3.shape[2]
    return pl.pallas_call(
        _mm_kernel,
        out_shape=jax.ShapeDtypeStruct((m, n), out_dtype),
        grid=(m // bm, n // bn),
        in_specs=[_act_spec(bm, k),
                  pl.BlockSpec((None, k, bn), lambda i, j: (layer, 0, j))],
        out_specs=pl.BlockSpec((bm, bn), lambda i, j: (i, j)),
        compiler_params=_params(("arbitrary", "arbitrary")),
        name=name,
    )(a, w3)


def matmul_bf16w(a, w2, bm, bn, out_dtype=BF16, name="matmul"):
    m, k = a.shape
    n = w2.shape[1]
    return pl.pallas_call(
        _mm_kernel,
        out_shape=jax.ShapeDtypeStruct((m, n), out_dtype),
        grid=(m // bm, n // bn),
        in_specs=[pl.BlockSpec((bm, k), lambda i, j: (i, 0)),
                  pl.BlockSpec((k, bn), lambda i, j: (0, j))],
        out_specs=pl.BlockSpec((bm, bn), lambda i, j: (i, j)),
        compiler_params=_params(("arbitrary", "arbitrary")),
        name=name,
    )(a, w2)


def _qkv_kernel(a_ref, w_ref, cos_ref, sin_ref, o_ref, *, n_q_blocks, n_rope_blocks, q_scale):
    j = pl.program_id(1)
    w = w_ref[...].astype(BF16)
    is_rope = j < n_rope_blocks
    mult = jnp.where(j < n_q_blocks, q_scale, 1.0).astype(F32)
    n_heads = w.shape[1] // HEAD_DIM
    part = a_ref.shape[0] // 2
    for r in range(2):
        rows = slice(r * part, (r + 1) * part)
        acc = jnp.dot(a_ref[rows, :], w, preferred_element_type=F32)
        cos = jnp.where(is_rope, cos_ref[rows, :] * mult, 1.0)
        sin = jnp.where(is_rope, sin_ref[rows, :] * mult, 0.0)
        for h in range(n_heads):
            t = acc[:, h * HEAD_DIM:(h + 1) * HEAD_DIM]
            rot = pltpu.roll(t, HEAD_DIM // 2, 1)
            o_ref[rows, h * HEAD_DIM:(h + 1) * HEAD_DIM] = (t * cos + rot * sin).astype(o_ref.dtype)


def qkv_projection(h2, w3, layer, cos, sin, q_dim, kv_dim, bm=2048, bn=512):
    m, k = h2.shape
    n = w3.shape[2]
    kern = functools.partial(_qkv_kernel, n_q_blocks=q_dim // bn, n_rope_blocks=(q_dim + kv_dim) // bn,
                             q_scale=LOG2E / math.sqrt(HEAD_DIM))
    tab = pl.BlockSpec((bm, HEAD_DIM), lambda i, j: (i, 0))
    return pl.pallas_call(
        kern,
        out_shape=jax.ShapeDtypeStruct((m, n), BF16),
        grid=(m // bm, n // bn),
        in_specs=[_act_spec(bm, k),
                  pl.BlockSpec((None, k, bn), lambda i, j: (layer, 0, j)),
                  tab, tab],
        out_specs=pl.BlockSpec((bm, bn), lambda i, j: (i, j)),
        compiler_params=_params(("arbitrary", "arbitrary")),
        name="qkv_projection",
    )(h2, w3, cos, sin)


def _gate_up_kernel(a_ref, wg_ref, wu_ref, wd_ref, o_ref, wdo_ref):
    a = a_ref[...]
    g = jnp.dot(a, wg_ref[...].astype(BF16), preferred_element_type=F32)
    u = jnp.dot(a, wu_ref[...].astype(BF16), preferred_element_type=F32)
    o_ref[...] = (g * jax.nn.sigmoid(g) * u).astype(o_ref.dtype)
    wdo_ref[...] = wd_ref[...].astype(wdo_ref.dtype)


def gate_up(h2, w3, w_down3, layer, d_ff, bm=2048, bn=256):
    m, k = h2.shape
    d_out = w_down3.shape[2]
    nblk = d_ff // bn
    steps = (m // bm) * nblk
    slab = d_ff // steps
    assert slab * steps == d_ff and slab % 16 == 0
    return pl.pallas_call(
        _gate_up_kernel,
        out_shape=(jax.ShapeDtypeStruct((m, d_ff), BF16), jax.ShapeDtypeStruct((d_ff, d_out), BF16)),
        grid=(m // bm, nblk),
        in_specs=[_act_spec(bm, k),
                  pl.BlockSpec((None, k, bn), lambda i, j: (layer, 0, j)),
                  pl.BlockSpec((None, k, bn), lambda i, j: (layer, 0, j + nblk)),
                  pl.BlockSpec((None, slab, d_out), lambda i, j: (layer, i * nblk + j, 0))],
        out_specs=(pl.BlockSpec((bm, bn), lambda i, j: (i, j)),
                   pl.BlockSpec((slab, d_out), lambda i, j: (i * nblk + j, 0))),
        compiler_params=_params(("arbitrary", "arbitrary")),
        name="gate_up",
    )(h2, w3, w3, w_down3)


def _scores_t(k, qs):
    return lax.dot_general(k, qs, (((1,), (1,)), ((), ())), preferred_element_type=F32)


def _head_rows(q):
    return jnp.concatenate([q[:, g * HEAD_DIM:(g + 1) * HEAD_DIM] for g in range(GROUP)], axis=0)


ATT_TQ = LANES
ATT_UNROLL = 8
ATT_SLOTS = 2


def _fill_band_masks(mask_ref, half_w):
    _, tk, tq = mask_ref.shape
    rel = (lax.broadcasted_iota(jnp.int32, (tk, tq), 1)
           - lax.broadcasted_iota(jnp.int32, (tk, tq), 0))
    for v in range(3):
        dist = rel + v * half_w
        mask_ref[v] = ((dist >= -half_w) & (dist <= half_w)).astype(jnp.int32)


def _window_start(q0, half_w, tk, seq_len):
    start = pl.multiple_of(jnp.clip(q0 - half_w, 0, seq_len - tk), 16)
    return start, (q0 - start) // half_w


def _tile_scores(q, k, st_buf):
    st_buf[...] = _scores_t(k, _head_rows(q))


def _tile_finish(v, mask, st_buf, p_buf, sinks=None):
    tq = mask.shape[1]
    keep = mask != 0
    dens, lses = [], []
    for g in range(GROUP):
        cols = slice(g * tq, (g + 1) * tq)
        s = jnp.where(keep, st_buf[:, cols], NEG_INF)
        m = jnp.max(s, axis=0, keepdims=True)
        if sinks is not None:
            m = jnp.maximum(m, sinks[g])
        p = jnp.exp2(s - m)
        den = jnp.sum(p, axis=0, keepdims=True)
        if sinks is not None:
            den = den + jnp.exp2(sinks[g] - m)
        p_buf[:, cols] = p.astype(BF16)
        dens.append(den)
        lses.append(m + jnp.log2(den))
    den = jnp.concatenate(dens, axis=1)
    ot = jnp.dot(v.T, p_buf[...], preferred_element_type=F32) / den
    return ot, jnp.concatenate(lses, axis=1)


def _run_tiles(n_tiles, scores, finish):
    assert n_tiles % ATT_UNROLL == 0 and ATT_UNROLL % ATT_SLOTS == 0
    scores(0, 0)

    def body(it, carry):
        for u in range(ATT_UNROLL):
            t = it * ATT_UNROLL + u
            scores(jnp.minimum(t + 1, n_tiles - 1), (u + 1) % ATT_SLOTS)
            finish(t, u % ATT_SLOTS)
        return carry

    lax.fori_loop(0, n_tiles // ATT_UNROLL, body, 0)


def _window_sink_kernel(sink_ref, q_ref, k_ref, v_ref, o_ref, mask_ref, st_buf, p_buf,
                        *, half_w, tk, seq_len):
    h = pl.program_id(1)
    tq = ATT_TQ
    _fill_band_masks(mask_ref, half_w)
    sinks = [sink_ref[0, h * GROUP + g] * LOG2E for g in range(GROUP)]

    def scores(t, slot):
        q0 = pl.multiple_of(t * tq, tq)
        start, _ = _window_start(q0, half_w, tk, seq_len)
        _tile_scores(q_ref[pl.ds(q0, tq), :], k_ref[pl.ds(start, tk), :], st_buf.at[slot])

    def finish(t, slot):
        q0 = pl.multiple_of(t * tq, tq)
        start, variant = _window_start(q0, half_w, tk, seq_len)
        ot, _ = _tile_finish(v_ref[pl.ds(start, tk), :], mask_ref[variant],
                             st_buf.at[slot], p_buf.at[slot], sinks)
        for g in range(GROUP):
            o_ref[pl.ds(q0, tq), g * HEAD_DIM:(g + 1) * HEAD_DIM] = (
                ot[:, g * tq:(g + 1) * tq].T.astype(o_ref.dtype))

    _run_tiles(seq_len // tq, scores, finish)


def window_sink_attention(qkv, half_w, n_kv_heads, sink):
    nb, seq_len, _ = qkv.shape
    tq = ATT_TQ
    tk = tq + 2 * half_w
    assert tq % half_w == 0 and (seq_len - tk) % half_w == 0 and seq_len % (tq * ATT_UNROLL) == 0
    n_q_cols = n_kv_heads * GROUP
    kern = functools.partial(_window_sink_kernel, half_w=half_w, tk=tk, seq_len=seq_len)
    return pl.pallas_call(
        kern,
        out_shape=jax.ShapeDtypeStruct((nb, seq_len, n_q_cols * HEAD_DIM), BF16),
        grid=(nb, n_kv_heads),
        in_specs=[
            pl.BlockSpec(memory_space=pltpu.SMEM),
            pl.BlockSpec((None, seq_len, GROUP * HEAD_DIM), lambda b, h: (b, 0, h)),
            pl.BlockSpec((None, seq_len, HEAD_DIM), lambda b, h: (b, 0, n_q_cols + h)),
            pl.BlockSpec((None, seq_len, HEAD_DIM), lambda b, h: (b, 0, n_q_cols + n_kv_heads + h)),
        ],
        out_specs=pl.BlockSpec((None, seq_len, GROUP * HEAD_DIM), lambda b, h: (b, 0, h)),
        scratch_shapes=[pltpu.VMEM((3, tk, tq), jnp.int32),
                        pltpu.VMEM((ATT_SLOTS, tk, GROUP * tq), F32),
                        pltpu.VMEM((ATT_SLOTS, tk, GROUP * tq), BF16)],
        compiler_params=_params(("arbitrary", "arbitrary")),
        name="window_sink_attention",
    )(sink.reshape(1, -1).astype(F32), qkv, qkv, qkv)


SUPER = 1024


def _dilate_kernel(x_ref, r4_ref, r16_ref, slab, slab4):
    for c in range(x_ref.shape[1] // LANES):
        cols = slice(c * LANES, (c + 1) * LANES)
        slab[...] = x_ref[:, cols].astype(F32)
        for res in range(4):
            cls = slab[pl.ds(res, SUPER // 4, stride=4), :]
            r4_ref[res, :, cols] = cls.astype(r4_ref.dtype)
            slab4[res] = cls
        for res in range(16):
            cls = slab4[res % 4, pl.ds(res // 4, SUPER // 16, stride=4), :]
            r16_ref[res, :, cols] = cls.astype(r16_ref.dtype)


def dilate(qkv, bw=512):
    nb, s, c = qkv.shape
    return pl.pallas_call(
        _dilate_kernel,
        out_shape=(jax.ShapeDtypeStruct((nb, 4, s // 4, c), BF16),
                   jax.ShapeDtypeStruct((nb, 16, s // 16, c), BF16)),
        grid=(nb, s // SUPER, c // bw),
        in_specs=[pl.BlockSpec((None, SUPER, bw), lambda b, i, j: (b, i, j))],
        out_specs=(pl.BlockSpec((None, 4, SUPER // 4, bw), lambda b, i, j: (b, 0, i, j)),
                   pl.BlockSpec((None, 16, SUPER // 16, bw), lambda b, i, j: (b, 0, i, j))),
        scratch_shapes=[pltpu.VMEM((SUPER, LANES), F32), pltpu.VMEM((4, SUPER // 4, LANES), F32)],
        compiler_params=_params(("parallel", "parallel", "parallel")),
        name="dilate",
    )(qkv)


DIL_HALF = 64
DIL_SUPER = 2048


def _dilated_kernel(q1_ref, k1_ref, v1_ref, q4_ref, k4_ref, v4_ref, q16_ref, k16_ref, v16_ref,
                    out_ref, mask_ref, st_buf, p_buf, o_scr, st1, st4, st16, *, seq_len):
    n = pl.program_id(2)
    tq, w = ATT_TQ, DIL_HALF
    tk = tq + 2 * w
    n_tiles = DIL_SUPER // tq
    _fill_band_masks(mask_ref, w)

    def stat_tile(lse2):
        rows = [lse2[:, g * tq:(g + 1) * tq] * LN2 for g in range(GROUP)]
        rows.append(jnp.zeros((LANES - GROUP, tq), F32))
        return jnp.concatenate(rows, axis=0).T

    def branch(seq, geometry, load_q, load_kv, out_rows, base, stat_ref):
        def scores(t, slot):
            q0, hnd = geometry(t)
            start, _ = _window_start(q0, w, tk, seq)
            _tile_scores(load_q(hnd), load_kv(0, hnd, start), st_buf.at[slot])

        def finish(t, slot):
            q0, hnd = geometry(t)
            start, variant = _window_start(q0, w, tk, seq)
            ot, lse2 = _tile_finish(load_kv(1, hnd, start), mask_ref[variant],
                                    st_buf.at[slot], p_buf.at[slot])
            rows = out_rows(hnd)
            for g in range(GROUP):
                o_scr[base + g, rows, :] = ot[:, g * tq:(g + 1) * tq].T
            stat_ref[rows, :] = stat_tile(lse2)

        _run_tiles(n_tiles, scores, finish)

    kv1 = (k1_ref, v1_ref)
    branch(seq_len,
           lambda t: (n * DIL_SUPER + t * tq, pl.multiple_of(t * tq, tq)),
           lambda r0: q1_ref[pl.ds(r0, tq), :],
           lambda i, r0, start: kv1[i][pl.ds(start, tk), :],
           lambda r0: pl.ds(r0, tq),
           0, st1)

    per4 = DIL_SUPER // 4
    kv4 = (k4_ref, v4_ref)
    branch(seq_len // 4,
           lambda t: (n * per4 + (t & 3) * tq, (t >> 2, pl.multiple_of((t & 3) * tq, tq))),
           lambda h: q4_ref[h[0], pl.ds(h[1], tq), :],
           lambda i, h, start: kv4[i][h[0], pl.ds(start, tk), :],
           lambda h: pl.ds(4 * h[1] + h[0], tq, stride=4),
           GROUP, st4)

    assert DIL_SUPER // 16 == tq
    kv16 = (k16_ref, v16_ref)
    branch(seq_len // 16,
           lambda t: (n * tq, t),
           lambda c: q16_ref[c],
           lambda i, c, start: kv16[i][c, pl.ds(start, tk), :],
           lambda c: pl.ds(c, tq, stride=16),
           2 * GROUP, st16)

    a1, a4, a16 = st1[...], st4[...], st16[...]
    mx = jnp.maximum(jnp.maximum(a1, a4), a16)
    e1 = jnp.exp(a1 - mx)
    e4 = jnp.exp(a4 - mx)
    inv = 1.0 / (e1 + e4 + jnp.exp(a16 - mx))
    alpha1 = e1 * inv
    alpha4 = e4 * inv
    for g in range(GROUP):
        o16 = o_scr[2 * GROUP + g]
        o = (o16 + alpha1[:, g:g + 1] * (o_scr[g] - o16)
             + alpha4[:, g:g + 1] * (o_scr[GROUP + g] - o16))
        out_ref[:, g * HEAD_DIM:(g + 1) * HEAD_DIM] = o.astype(out_ref.dtype)


def dilated_attention(nat, r4, r16, n_kv_heads):
    nb, s, c = nat.shape
    gw = GROUP * HEAD_DIM
    nqc = n_kv_heads * GROUP
    kcol = lambda h: nqc + h
    vcol = lambda h: nqc + n_kv_heads + h
    sup, tq = DIL_SUPER, ATT_TQ
    tk = tq + 2 * DIL_HALF
    assert s % sup == 0 and s // 16 >= tk and tq % DIL_HALF == 0
    in_specs = [
        pl.BlockSpec((None, sup, gw), lambda b, h, n: (b, n, h)),
        pl.BlockSpec((None, s, HEAD_DIM), lambda b, h, n: (b, 0, kcol(h))),
        pl.BlockSpec((None, s, HEAD_DIM), lambda b, h, n: (b, 0, vcol(h))),
        pl.BlockSpec((None, 4, sup // 4, gw), lambda b, h, n: (b, 0, n, h)),
        pl.BlockSpec((None, 4, s // 4, HEAD_DIM), lambda b, h, n: (b, 0, 0, kcol(h))),
        pl.BlockSpec((None, 4, s // 4, HEAD_DIM), lambda b, h, n: (b, 0, 0, vcol(h))),
        pl.BlockSpec((None, 16, sup // 16, gw), lambda b, h, n: (b, 0, n, h)),
        pl.BlockSpec((None, 16, s // 16, HEAD_DIM), lambda b, h, n: (b, 0, 0, kcol(h))),
        pl.BlockSpec((None, 16, s // 16, HEAD_DIM), lambda b, h, n: (b, 0, 0, vcol(h))),
    ]
    return pl.pallas_call(
        functools.partial(_dilated_kernel, seq_len=s),
        out_shape=jax.ShapeDtypeStruct((nb, s, nqc * HEAD_DIM), BF16),
        grid=(nb, n_kv_heads, s // sup),
        in_specs=in_specs,
        out_specs=pl.BlockSpec((None, sup, gw), lambda b, h, n: (b, n, h)),
        scratch_shapes=[pltpu.VMEM((3, tk, tq), jnp.int32),
                        pltpu.VMEM((ATT_SLOTS, tk, GROUP * tq), F32),
                        pltpu.VMEM((ATT_SLOTS, tk, GROUP * tq), BF16),
                        pltpu.VMEM((3 * GROUP, sup, LANES), F32),
                        pltpu.VMEM((sup, LANES), F32),
                        pltpu.VMEM((sup, LANES), F32),
                        pltpu.VMEM((sup, LANES), F32)],
        compiler_params=_params(("arbitrary", "arbitrary", "arbitrary")),
        name="dilated_attention",
    )(nat, nat, nat, r4, r4, r4, r16, r16, r16)


def kernel(x, c, positions, w_qkv, w_o, sink, w_gate_up, w_down, g_mix_pre, g_mix_post,
           g_ffn_pre, g_ffn_post, w_ada, b_ada):
    nb, s, d = x.shape
    depth = w_qkv.shape[0]
    t = nb * s
    q_dim = w_o.shape[1]
    kv_dim = (w_qkv.shape[2] - q_dim) // 2
    n_kv_heads = kv_dim // HEAD_DIM
    d_ff = w_down.shape[1]
    assert [r for _, r in DILATED_BRANCHES] == [1, 4, 16] and all(w // (2 * r) == 64 for w, r in DILATED_BRANCHES)

    c_pad = jnp.pad(c, ((0, 8 - nb), (0, 0)))
    mod = ada_modulation(c_pad, w_ada, b_ada)
    mod4 = mod.reshape(depth, 8, 1, 6 * d)
    cos, sin = rope_tables(positions.reshape(t, 1))

    g_mix_pre3 = g_mix_pre.reshape(depth, 1, d)
    g_mix_post3 = g_mix_post.reshape(depth, 1, d)
    g_ffn_pre3 = g_ffn_pre.reshape(depth, 1, d)
    g_ffn_post3 = g_ffn_post.reshape(depth, 1, d)
    SH_T, SC_T, GT_T, SH_F, SC_F, GT_F = range(6)

    h = prenorm(x, g_mix_pre3, mod4, 0, SC_T, SH_T)
    for i in range(depth):
        h2 = h.reshape(t, d)
        qkv = qkv_projection(h2, w_qkv, i, cos, sin, q_dim, kv_dim).reshape(nb, s, -1)
        if i % N_MIXERS == 0:
            att = window_sink_attention(qkv, WINDOW_A_HALF, n_kv_heads, sink[i // N_MIXERS])
        else:
            r4, r16 = dilate(qkv)
            att = dilated_attention(qkv, r4, r16, n_kv_heads)
        y = matmul_f32w(att.reshape(t, q_dim), w_o, i, bm=2048, bn=512, name="wo_projection")
        x, h = post_residual(x, y.reshape(nb, s, d), g_mix_post3, mod4, i, GT_T,
                             nxt=(g_ffn_pre3, i, SC_F, SH_F))
        a, wd_b = gate_up(h.reshape(t, d), w_gate_up, w_down, i, d_ff)
        y = matmul_bf16w(a, wd_b, bm=512, bn=512, name="down_projection")
        nxt = (g_mix_pre3, i + 1, SC_T, SH_T) if i + 1 < depth else None
        x, h = post_residual(x, y.reshape(nb, s, d), g_ffn_post3, mod4, i, GT_F, nxt=nxt)
    return x
```

```python
import functools
import math

import jax
import jax.numpy as jnp
from jax import lax
from jax.experimental import pallas as pl
from jax.experimental.pallas import tpu as pltpu

HEAD_DIM = 128
GROUP = 4
ROPE_THETA = 10000.0
WINDOW_A_HALF = 128
DILATED_BRANCHES = ((128, 1), (512, 4), (2048, 16))
RMS_EPS = 1e-6
NEG_INF = -1e30
N_MIXERS = 2
LOG2E = 1.4426950408889634
LN2 = 0.6931471805599453

LANES = 128
VMEM_LIMIT = 58 * 1024 * 1024

F32 = jnp.float32
BF16 = jnp.bfloat16


def _params(sem, vmem=VMEM_LIMIT):
    return pltpu.CompilerParams(dimension_semantics=sem, vmem_limit_bytes=vmem)


def _ada_block(c_ref, w_ref, b_ref):
    return jnp.dot(c_ref[...].astype(BF16), w_ref[...].astype(BF16), preferred_element_type=F32) + b_ref[...]


def _ada_kernel(c_ref, w_ref, b_ref, o_ref):
    o_ref[...] = _ada_block(c_ref, w_ref, b_ref)


def ada_modulation(c_pad, w_ada, b_ada3, layer, bn=1024):
    _, d, n = w_ada.shape
    rows = c_pad.shape[0]
    return pl.pallas_call(
        _ada_kernel,
        out_shape=jax.ShapeDtypeStruct((rows, n), F32),
        grid=(n // bn,),
        in_specs=[
            pl.BlockSpec((rows, d), lambda j: (0, 0)),
            pl.BlockSpec((None, d, bn), lambda j: (layer, 0, j)),
            pl.BlockSpec((None, 1, bn), lambda j: (layer, 0, j)),
        ],
        out_specs=pl.BlockSpec((rows, bn), lambda j: (0, j)),
        compiler_params=_params(("parallel",)),
        name="ada_modulation",
    )(c_pad, w_ada, b_ada3)


def _rope_table_kernel(pos_ref, cos_ref, sin_ref):
    half = HEAD_DIM // 2
    pos = pos_ref[...].astype(F32)
    lane = lax.broadcasted_iota(jnp.int32, (1, HEAD_DIM), 1)
    fidx = (lane & (half - 1)).astype(F32)
    inv_freq = jnp.exp(-math.log(ROPE_THETA) * fidx / half)
    ang = pos * inv_freq
    cos_ref[...] = jnp.cos(ang)
    s = jnp.sin(ang)
    sin_ref[...] = jnp.where(lane < half, -s, s)


def rope_tables(pos_col, bm=1024):
    t = pos_col.shape[0]
    out = jax.ShapeDtypeStruct((t, HEAD_DIM), F32)
    return pl.pallas_call(
        _rope_table_kernel,
        out_shape=(out, out),
        grid=(t // bm,),
        in_specs=[pl.BlockSpec((bm, 1), lambda i: (i, 0))],
        out_specs=(pl.BlockSpec((bm, HEAD_DIM), lambda i: (i, 0)),
                   pl.BlockSpec((bm, HEAD_DIM), lambda i: (i, 0))),
        compiler_params=_params(("parallel",)),
        name="rope_tables",
    )(pos_col)


def _rms(x, g):
    ms = jnp.mean(x * x, axis=-1, keepdims=True)
    return x * lax.rsqrt(ms + RMS_EPS) * g


def _prenorm_kernel(x_ref, g_ref, sc_ref, sh_ref, h_ref):
    y = _rms(x_ref[...], g_ref[...])
    h_ref[...] = (y * (1.0 + sc_ref[...]) + sh_ref[...]).astype(h_ref.dtype)


def _mod_spec(chunk, d):
    return pl.BlockSpec((None, 1, d), lambda b, i: (b, 0, chunk))


def _vec_spec(layer, d):
    return pl.BlockSpec((None, 1, d), lambda b, i: (layer, 0, 0))


NORM_ROWS = 512


def prenorm(x, g3, layer, mod, sc_chunk, sh_chunk, bm=NORM_ROWS):
    nb, s, d = x.shape
    row = pl.BlockSpec((None, bm, d), lambda b, i: (b, i, 0))
    return pl.pallas_call(
        _prenorm_kernel,
        out_shape=jax.ShapeDtypeStruct(x.shape, BF16),
        grid=(nb, s // bm),
        in_specs=[row, _vec_spec(layer, d), _mod_spec(sc_chunk, d), _mod_spec(sh_chunk, d)],
        out_specs=row,
        compiler_params=_params(("parallel", "parallel")),
        name="prenorm",
    )(x, g3, mod, mod)


def _post_kernel(x_ref, y_ref, gpost_ref, gate_ref, xo_ref):
    yn = _rms(y_ref[...].astype(F32), gpost_ref[...])
    xo_ref[...] = x_ref[...] + gate_ref[...] * yn


def _post_next_kernel(x_ref, y_ref, gpost_ref, gate_ref, gpre_ref, sc_ref, sh_ref, xo_ref, h_ref):
    yn = _rms(y_ref[...].astype(F32), gpost_ref[...])
    xn = x_ref[...] + gate_ref[...] * yn
    xo_ref[...] = xn
    hn = _rms(xn, gpre_ref[...])
    h_ref[...] = (hn * (1.0 + sc_ref[...]) + sh_ref[...]).astype(h_ref.dtype)


def post_residual(x, y, gpost3, layer, mod, gate_chunk, nxt=None, bm=NORM_ROWS):
    nb, s, d = x.shape
    row = pl.BlockSpec((None, bm, d), lambda b, i: (b, i, 0))
    in_specs = [row, row, _vec_spec(layer, d), _mod_spec(gate_chunk, d)]
    args = [x, y, gpost3, mod]
    if nxt is None:
        return pl.pallas_call(
            _post_kernel,
            out_shape=jax.ShapeDtypeStruct(x.shape, F32),
            grid=(nb, s // bm),
            in_specs=in_specs,
            out_specs=row,
            compiler_params=_params(("parallel", "parallel")),
            name="post_residual",
        )(*args), None
    g3n, ln, modn, scn, shn = nxt
    in_specs += [_vec_spec(ln, d), _mod_spec(scn, d), _mod_spec(shn, d)]
    args += [g3n, modn, modn]
    return pl.pallas_call(
        _post_next_kernel,
        out_shape=(jax.ShapeDtypeStruct(x.shape, F32), jax.ShapeDtypeStruct(x.shape, BF16)),
        grid=(nb, s // bm),
        in_specs=in_specs,
        out_specs=(row, row),
        compiler_params=_params(("parallel", "parallel")),
        name="post_residual_next",
    )(*args)


def _act_spec(bm, k):
    return pl.BlockSpec((bm, k), lambda i, j: (i, 0), pipeline_mode=pl.Buffered(1))


def _mm_kernel(a_ref, w_ref, o_ref):
    w = w_ref[...].astype(BF16)
    o_ref[...] = jnp.dot(a_ref[...], w, preferred_element_type=F32).astype(o_ref.dtype)


def matmul_f32w(a, w3, layer, bm, bn, out_dtype=BF16, name="matmul"):
    m, k = a.shape
    n = w3.shape[2]
    return pl.pallas_call(
        _mm_kernel,
        out_shape=jax.ShapeDtypeStruct((m, n), out_dtype),
        grid=(m // bm, n // bn),
        in_specs=[_act_spec(bm, k),
                  pl.BlockSpec((None, k, bn), lambda i, j: (layer, 0, j))],
        out_specs=pl.BlockSpec((bm, bn), lambda i, j: (i, j)),
        compiler_params=_params(("arbitrary", "arbitrary")),
        name=name,
    )(a, w3)


def matmul_bf16w(a, w2, bm, bn, out_dtype=BF16, name="matmul"):
    m, k = a.shape
    n = w2.shape[1]
    return pl.pallas_call(
        _mm_kernel,
        out_shape=jax.ShapeDtypeStruct((m, n), out_dtype),
        grid=(m // bm, n // bn),
        in_specs=[pl.BlockSpec((bm, k), lambda i, j: (i, 0)),
                  pl.BlockSpec((k, bn), lambda i, j: (0, j))],
        out_specs=pl.BlockSpec((bm, bn), lambda i, j: (i, j)),
        compiler_params=_params(("arbitrary", "arbitrary")),
        name=name,
    )(a, w2)


def _qkv_kernel(a_ref, w_ref, cos_ref, sin_ref, o_ref, *, n_q_blocks, n_rope_blocks, q_scale):
    j = pl.program_id(1)
    w = w_ref[...].astype(BF16)
    is_rope = j < n_rope_blocks
    mult = jnp.where(j < n_q_blocks, q_scale, 1.0).astype(F32)
    n_heads = w.shape[1] // HEAD_DIM
    part = a_ref.shape[0] // 2
    for r in range(2):
        rows = slice(r * part, (r + 1) * part)
        acc = jnp.dot(a_ref[rows, :], w, preferred_element_type=F32)
        cos = jnp.where(is_rope, cos_ref[rows, :] * mult, 1.0)
        sin = jnp.where(is_rope, sin_ref[rows, :] * mult, 0.0)
        for h in range(n_heads):
            t = acc[:, h * HEAD_DIM:(h + 1) * HEAD_DIM]
            rot = pltpu.roll(t, HEAD_DIM // 2, 1)
            o_ref[rows, h * HEAD_DIM:(h + 1) * HEAD_DIM] = (t * cos + rot * sin).astype(o_ref.dtype)


def qkv_projection(h2, w3, layer, cos, sin, q_dim, kv_dim, bm=2048, bn=512):
    m, k = h2.shape
    n = w3.shape[2]
    kern = functools.partial(_qkv_kernel, n_q_blocks=q_dim // bn, n_rope_blocks=(q_dim + kv_dim) // bn,
                             q_scale=LOG2E / math.sqrt(HEAD_DIM))
    tab = pl.BlockSpec((bm, HEAD_DIM), lambda i, j: (i, 0))
    return pl.pallas_call(
        kern,
        out_shape=jax.ShapeDtypeStruct((m, n), BF16),
        grid=(m // bm, n // bn),
        in_specs=[_act_spec(bm, k),
                  pl.BlockSpec((None, k, bn), lambda i, j: (layer, 0, j)),
                  tab, tab],
        out_specs=pl.BlockSpec((bm, bn), lambda i, j: (i, j)),
        compiler_params=_params(("arbitrary", "arbitrary")),
        name="qkv_projection",
    )(h2, w3, cos, sin)


ADA_SIDE_BN = 256


def _gate_up_kernel(a_ref, wg_ref, wu_ref, wd_ref, *rest, with_ada):
    if with_ada:
        c_ref, wa_ref, ba_ref, o_ref, wdo_ref, mod_ref = rest
    else:
        o_ref, wdo_ref = rest
    a = a_ref[...]
    g = jnp.dot(a, wg_ref[...].astype(BF16), preferred_element_type=F32)
    u = jnp.dot(a, wu_ref[...].astype(BF16), preferred_element_type=F32)
    o_ref[...] = (g * jax.nn.sigmoid(g) * u).astype(o_ref.dtype)
    wdo_ref[...] = wd_ref[...].astype(wdo_ref.dtype)
    if with_ada:
        mod_ref[...] = _ada_block(c_ref, wa_ref, ba_ref)


def gate_up(h2, w3, w_down3, layer, d_ff, ada_next=None, bm=2048, bn=256):
    m, k = h2.shape
    d_out = w_down3.shape[2]
    nblk = d_ff // bn
    steps = (m // bm) * nblk
    slab = d_ff // steps
    assert slab * steps == d_ff and slab % 16 == 0
    step = lambda i, j: i * nblk + j
    in_specs = [_act_spec(bm, k),
                pl.BlockSpec((None, k, bn), lambda i, j: (layer, 0, j)),
                pl.BlockSpec((None, k, bn), lambda i, j: (layer, 0, j + nblk)),
                pl.BlockSpec((None, slab, d_out), lambda i, j: (layer, step(i, j), 0))]
    out_shape = [jax.ShapeDtypeStruct((m, d_ff), BF16), jax.ShapeDtypeStruct((d_ff, d_out), BF16)]
    out_specs = [pl.BlockSpec((bm, bn), lambda i, j: (i, j)),
                 pl.BlockSpec((slab, d_out), lambda i, j: (step(i, j), 0))]
    args = [h2, w3, w3, w_down3]
    if ada_next is not None:
        c_pad, w_ada, b_ada3 = ada_next
        rows, n_mod = c_pad.shape[0], w_ada.shape[2]
        n_ada = n_mod // ADA_SIDE_BN
        assert n_ada <= steps
        ablk = lambda i, j: jnp.minimum(step(i, j), n_ada - 1)
        in_specs += [pl.BlockSpec((rows, k), lambda i, j: (0, 0)),
                     pl.BlockSpec((None, k, ADA_SIDE_BN), lambda i, j: (layer + 1, 0, ablk(i, j))),
                     pl.BlockSpec((None, 1, ADA_SIDE_BN), lambda i, j: (layer + 1, 0, ablk(i, j)))]
        out_shape.append(jax.ShapeDtypeStruct((rows, n_mod), F32))
        out_specs.append(pl.BlockSpec((rows, ADA_SIDE_BN), lambda i, j: (0, ablk(i, j))))
        args += [c_pad, w_ada, b_ada3]
    return pl.pallas_call(
        functools.partial(_gate_up_kernel, with_ada=ada_next is not None),
        out_shape=tuple(out_shape),
        grid=(m // bm, nblk),
        in_specs=in_specs,
        out_specs=tuple(out_specs),
        compiler_params=_params(("arbitrary", "arbitrary")),
        name="gate_up",
    )(*args)


def _scores_t(k, qs):
    return lax.dot_general(k, qs, (((1,), (1,)), ((), ())), preferred_element_type=F32)


def _head_rows(q):
    return jnp.concatenate([q[:, g * HEAD_DIM:(g + 1) * HEAD_DIM] for g in range(GROUP)], axis=0)


ATT_TQ = LANES
ATT_UNROLL = 8
ATT_SLOTS = 2


def _fill_band_masks(mask_ref, half_w):
    _, tk, tq = mask_ref.shape
    rel = (lax.broadcasted_iota(jnp.int32, (tk, tq), 1)
           - lax.broadcasted_iota(jnp.int32, (tk, tq), 0))
    for v in range(3):
        dist = rel + v * half_w
        mask_ref[v] = ((dist >= -half_w) & (dist <= half_w)).astype(jnp.int32)


def _window_start(q0, half_w, tk, seq_len):
    start = pl.multiple_of(jnp.clip(q0 - half_w, 0, seq_len - tk), 16)
    return start, (q0 - start) // half_w


def _tile_scores(q, k, st_buf):
    st_buf[...] = _scores_t(k, _head_rows(q))


def _tile_finish(v, mask, st_buf, p_buf, sinks=None):
    tq = mask.shape[1]
    keep = mask != 0
    dens, lses = [], []
    for g in range(GROUP):
        cols = slice(g * tq, (g + 1) * tq)
        s = jnp.where(keep, st_buf[:, cols], NEG_INF)
        m = jnp.max(s, axis=0, keepdims=True)
        if sinks is not None:
            m = jnp.maximum(m, sinks[g])
        p = jnp.exp2(s - m)
        den = jnp.sum(p, axis=0, keepdims=True)
        if sinks is not None:
            den = den + jnp.exp2(sinks[g] - m)
        p_buf[:, cols] = p.astype(BF16)
        dens.append(den)
        lses.append(m + jnp.log2(den))
    den = jnp.concatenate(dens, axis=1)
    ot = jnp.dot(v.T, p_buf[...], preferred_element_type=F32) / den
    return ot, jnp.concatenate(lses, axis=1)


def _run_tiles(n_tiles, scores, finish):
    assert n_tiles % ATT_UNROLL == 0 and ATT_UNROLL % ATT_SLOTS == 0
    scores(0, 0)

    def body(it, carry):
        for u in range(ATT_UNROLL):
            t = it * ATT_UNROLL + u
            scores(jnp.minimum(t + 1, n_tiles - 1), (u + 1) % ATT_SLOTS)
            finish(t, u % ATT_SLOTS)
        return carry

    lax.fori_loop(0, n_tiles // ATT_UNROLL, body, 0)


def _window_sink_kernel(sink_ref, q_ref, k_ref, v_ref, o_ref, mask_ref, st_buf, p_buf,
                        *, half_w, tk, seq_len):
    h = pl.program_id(1)
    tq = ATT_TQ
    _fill_band_masks(mask_ref, half_w)
    sinks = [sink_ref[0, h * GROUP + g] * LOG2E for g in range(GROUP)]

    def scores(t, slot):
        q0 = pl.multiple_of(t * tq, tq)
        start, _ = _window_start(q0, half_w, tk, seq_len)
        _tile_scores(q_ref[pl.ds(q0, tq), :], k_ref[pl.ds(start, tk), :], st_buf.at[slot])

    def finish(t, slot):
        q0 = pl.multiple_of(t * tq, tq)
        start, variant = _window_start(q0, half_w, tk, seq_len)
        ot, _ = _tile_finish(v_ref[pl.ds(start, tk), :], mask_ref[variant],
                             st_buf.at[slot], p_buf.at[slot], sinks)
        for g in range(GROUP):
            o_ref[pl.ds(q0, tq), g * HEAD_DIM:(g + 1) * HEAD_DIM] = (
                ot[:, g * tq:(g + 1) * tq].T.astype(o_ref.dtype))

    _run_tiles(seq_len // tq, scores, finish)


def window_sink_attention(qkv, half_w, n_kv_heads, sink):
    nb, seq_len, _ = qkv.shape
    tq = ATT_TQ
    tk = tq + 2 * half_w
    assert tq % half_w == 0 and (seq_len - tk) % half_w == 0 and seq_len % (tq * ATT_UNROLL) == 0
    n_q_cols = n_kv_heads * GROUP
    kern = functools.partial(_window_sink_kernel, half_w=half_w, tk=tk, seq_len=seq_len)
    return pl.pallas_call(
        kern,
        out_shape=jax.ShapeDtypeStruct((nb, seq_len, n_q_cols * HEAD_DIM), BF16),
        grid=(nb, n_kv_heads),
        in_specs=[
            pl.BlockSpec(memory_space=pltpu.SMEM),
            pl.BlockSpec((None, seq_len, GROUP * HEAD_DIM), lambda b, h: (b, 0, h)),
            pl.BlockSpec((None, seq_len, HEAD_DIM), lambda b, h: (b, 0, n_q_cols + h)),
            pl.BlockSpec((None, seq_len, HEAD_DIM), lambda b, h: (b, 0, n_q_cols + n_kv_heads + h)),
        ],
        out_specs=pl.BlockSpec((None, seq_len, GROUP * HEAD_DIM), lambda b, h: (b, 0, h)),
        scratch_shapes=[pltpu.VMEM((3, tk, tq), jnp.int32),
                        pltpu.VMEM((ATT_SLOTS, tk, GROUP * tq), F32),
                        pltpu.VMEM((ATT_SLOTS, tk, GROUP * tq), BF16)],
        compiler_params=_params(("arbitrary", "arbitrary")),
        name="window_sink_attention",
    )(sink.reshape(1, -1).astype(F32), qkv, qkv, qkv)


SUPER = 1024


def _dilate_kernel(x_ref, r4_ref, r16_ref, slab, slab4):
    for c in range(x_ref.shape[1] // LANES):
        cols = slice(c * LANES, (c + 1) * LANES)
        slab[...] = x_ref[:, cols].astype(F32)
        for res in range(4):
            cls = slab[pl.ds(res, SUPER // 4, stride=4), :]
            r4_ref[res, :, cols] = cls.astype(r4_ref.dtype)
            slab4[res] = cls
        for res in range(16):
            cls = slab4[res % 4, pl.ds(res // 4, SUPER // 16, stride=4), :]
            r16_ref[res, :, cols] = cls.astype(r16_ref.dtype)


def dilate(qkv, bw=512):
    nb, s, c = qkv.shape
    return pl.pallas_call(
        _dilate_kernel,
        out_shape=(jax.ShapeDtypeStruct((nb, 4, s // 4, c), BF16),
                   jax.ShapeDtypeStruct((nb, 16, s // 16, c), BF16)),
        grid=(nb, s // SUPER, c // bw),
        in_specs=[pl.BlockSpec((None, SUPER, bw), lambda b, i, j: (b, i, j))],
        out_specs=(pl.BlockSpec((None, 4, SUPER // 4, bw), lambda b, i, j: (b, 0, i, j)),
                   pl.BlockSpec((None, 16, SUPER // 16, bw), lambda b, i, j: (b, 0, i, j))),
        scratch_shapes=[pltpu.VMEM((SUPER, LANES), F32), pltpu.VMEM((4, SUPER // 4, LANES), F32)],
        compiler_params=_params(("parallel", "parallel", "parallel")),
        name="dilate",
    )(qkv)


DIL_HALF = 64
DIL_SUPER = 2048


def _dilated_kernel(q1_ref, k1_ref, v1_ref, q4_ref, k4_ref, v4_ref, q16_ref, k16_ref, v16_ref,
                    out_ref, mask_ref, st_buf, p_buf, o_scr, st1, st4, st16, *, seq_len):
    n = pl.program_id(2)
    tq, w = ATT_TQ, DIL_HALF
    tk = tq + 2 * w
    n_tiles = DIL_SUPER // tq
    _fill_band_masks(mask_ref, w)

    def stat_tile(lse2):
        rows = [lse2[:, g * tq:(g + 1) * tq] * LN2 for g in range(GROUP)]
        rows.append(jnp.zeros((LANES - GROUP, tq), F32))
        return jnp.concatenate(rows, axis=0).T

    def branch(seq, geometry, load_q, load_kv, out_rows, base, stat_ref):
        def scores(t, slot):
            q0, hnd = geometry(t)
            start, _ = _window_start(q0, w, tk, seq)
            _tile_scores(load_q(hnd), load_kv(0, hnd, start), st_buf.at[slot])

        def finish(t, slot):
            q0, hnd = geometry(t)
            start, variant = _window_start(q0, w, tk, seq)
            ot, lse2 = _tile_finish(load_kv(1, hnd, start), mask_ref[variant],
                                    st_buf.at[slot], p_buf.at[slot])
            rows = out_rows(hnd)
            for g in range(GROUP):
                o_scr[base + g, rows, :] = ot[:, g * tq:(g + 1) * tq].T
            stat_ref[rows, :] = stat_tile(lse2)

        _run_tiles(n_tiles, scores, finish)

    kv1 = (k1_ref, v1_ref)
    branch(seq_len,
           lambda t: (n * DIL_SUPER + t * tq, pl.multiple_of(t * tq, tq)),
           lambda r0: q1_ref[pl.ds(r0, tq), :],
           lambda i, r0, start: kv1[i][pl.ds(start, tk), :],
           lambda r0: pl.ds(r0, tq),
           0, st1)

    per4 = DIL_SUPER // 4
    kv4 = (k4_ref, v4_ref)
    branch(seq_len // 4,
           lambda t: (n * per4 + (t & 3) * tq, (t >> 2, pl.multiple_of((t & 3) * tq, tq))),
           lambda h: q4_ref[h[0], pl.ds(h[1], tq), :],
           lambda i, h, start: kv4[i][h[0], pl.ds(start, tk), :],
           lambda h: pl.ds(4 * h[1] + h[0], tq, stride=4),
           GROUP, st4)

    assert DIL_SUPER // 16 == tq
    kv16 = (k16_ref, v16_ref)
    branch(seq_len // 16,
           lambda t: (n * tq, t),
           lambda c: q16_ref[c],
           lambda i, c, start: kv16[i][c, pl.ds(start, tk), :],
           lambda c: pl.ds(c, tq, stride=16),
           2 * GROUP, st16)

    a1, a4, a16 = st1[...], st4[...], st16[...]
    mx = jnp.maximum(jnp.maximum(a1, a4), a16)
    e1 = jnp.exp(a1 - mx)
    e4 = jnp.exp(a4 - mx)
    inv = 1.0 / (e1 + e4 + jnp.exp(a16 - mx))
    alpha1 = e1 * inv
    alpha4 = e4 * inv
    for g in range(GROUP):
        o16 = o_scr[2 * GROUP + g]
        o = (o16 + alpha1[:, g:g + 1] * (o_scr[g] - o16)
             + alpha4[:, g:g + 1] * (o_scr[GROUP + g] - o16))
        out_ref[:, g * HEAD_DIM:(g + 1) * HEAD_DIM] = o.astype(out_ref.dtype)


def dilated_attention(nat, r4, r16, n_kv_heads):
    nb, s, c = nat.shape
    gw = GROUP * HEAD_DIM
    nqc = n_kv_heads * GROUP
    kcol = lambda h: nqc + h
    vcol = lambda h: nqc + n_kv_heads + h
    sup, tq = DIL_SUPER, ATT_TQ
    tk = tq + 2 * DIL_HALF
    assert s % sup == 0 and s // 16 >= tk and tq % DIL_HALF == 0
    in_specs = [
        pl.BlockSpec((None, sup, gw), lambda b, h, n: (b, n, h)),
        pl.BlockSpec((None, s, HEAD_DIM), lambda b, h, n: (b, 0, kcol(h))),
        pl.BlockSpec((None, s, HEAD_DIM), lambda b, h, n: (b, 0, vcol(h))),
        pl.BlockSpec((None, 4, sup // 4, gw), lambda b, h, n: (b, 0, n, h)),
        pl.BlockSpec((None, 4, s // 4, HEAD_DIM), lambda b, h, n: (b, 0, 0, kcol(h))),
        pl.BlockSpec((None, 4, s // 4, HEAD_DIM), lambda b, h, n: (b, 0, 0, vcol(h))),
        pl.BlockSpec((None, 16, sup // 16, gw), lambda b, h, n: (b, 0, n, h)),
        pl.BlockSpec((None, 16, s // 16, HEAD_DIM), lambda b, h, n: (b, 0, 0, kcol(h))),
        pl.BlockSpec((None, 16, s // 16, HEAD_DIM), lambda b, h, n: (b, 0, 0, vcol(h))),
    ]
    return pl.pallas_call(
        functools.partial(_dilated_kernel, seq_len=s),
        out_shape=jax.ShapeDtypeStruct((nb, s, nqc * HEAD_DIM), BF16),
        grid=(nb, n_kv_heads, s // sup),
        in_specs=in_specs,
        out_specs=pl.BlockSpec((None, sup, gw), lambda b, h, n: (b, n, h)),
        scratch_shapes=[pltpu.VMEM((3, tk, tq), jnp.int32),
                        pltpu.VMEM((ATT_SLOTS, tk, GROUP * tq), F32),
                        pltpu.VMEM((ATT_SLOTS, tk, GROUP * tq), BF16),
                        pltpu.VMEM((3 * GROUP, sup, LANES), F32),
                        pltpu.VMEM((sup, LANES), F32),
                        pltpu.VMEM((sup, LANES), F32),
                        pltpu.VMEM((sup, LANES), F32)],
        compiler_params=_params(("arbitrary", "arbitrary", "arbitrary")),
        name="dilated_attention",
    )(nat, nat, nat, r4, r4, r4, r16, r16, r16)


def kernel(x, c, positions, w_qkv, w_o, sink, w_gate_up, w_down, g_mix_pre, g_mix_post,
           g_ffn_pre, g_ffn_post, w_ada, b_ada):
    nb, s, d = x.shape
    depth = w_qkv.shape[0]
    t = nb * s
    q_dim = w_o.shape[1]
    kv_dim = (w_qkv.shape[2] - q_dim) // 2
    n_kv_heads = kv_dim // HEAD_DIM
    d_ff = w_down.shape[1]
    assert [r for _, r in DILATED_BRANCHES] == [1, 4, 16] and all(w // (2 * r) == 64 for w, r in DILATED_BRANCHES)

    c_pad = jnp.pad(c, ((0, 8 - nb), (0, 0)))
    b_ada3 = b_ada.reshape(depth, 1, 6 * d)
    as_mod = lambda m2: m2.reshape(8, 1, 6 * d)
    mod = as_mod(ada_modulation(c_pad, w_ada, b_ada3, 0))
    cos, sin = rope_tables(positions.reshape(t, 1))

    g_mix_pre3 = g_mix_pre.reshape(depth, 1, d)
    g_mix_post3 = g_mix_post.reshape(depth, 1, d)
    g_ffn_pre3 = g_ffn_pre.reshape(depth, 1, d)
    g_ffn_post3 = g_ffn_post.reshape(depth, 1, d)
    SH_T, SC_T, GT_T, SH_F, SC_F, GT_F = range(6)

    h = prenorm(x, g_mix_pre3, 0, mod, SC_T, SH_T)
    for i in range(depth):
        last = i + 1 == depth
        h2 = h.reshape(t, d)
        qkv = qkv_projection(h2, w_qkv, i, cos, sin, q_dim, kv_dim).reshape(nb, s, -1)
        if i % N_MIXERS == 0:
            att = window_sink_attention(qkv, WINDOW_A_HALF, n_kv_heads, sink[i // N_MIXERS])
        else:
            r4, r16 = dilate(qkv)
            att = dilated_attention(qkv, r4, r16, n_kv_heads)
        y = matmul_f32w(att.reshape(t, q_dim), w_o, i, bm=2048, bn=512, name="wo_projection")
        x, h = post_residual(x, y.reshape(nb, s, d), g_mix_post3, i, mod, GT_T,
                             nxt=(g_ffn_pre3, i, mod, SC_F, SH_F))
        outs = gate_up(h.reshape(t, d), w_gate_up, w_down, i, d_ff,
                       ada_next=None if last else (c_pad, w_ada, b_ada3))
        a, wd_b = outs[:2]
        y = matmul_bf16w(a, wd_b, bm=512, bn=512, name="down_projection")
        mod_next = None if last else as_mod(outs[2])
        nxt = None if last else (g_mix_pre3, i + 1, mod_next, SC_T, SH_T)
        x, h = post_residual(x, y.reshape(nb, s, d), g_ffn_post3, i, mod, GT_F, nxt=nxt)
        mod = mod_next
    return x
```

```python
import functools
import math

import jax
import jax.numpy as jnp
from jax import lax
from jax.experimental import pallas as pl
from jax.experimental.pallas import tpu as pltpu

HEAD_DIM = 128
GROUP = 4
ROPE_THETA = 10000.0
WINDOW_A_HALF = 128
DILATED_BRANCHES = ((128, 1), (512, 4), (2048, 16))
RMS_EPS = 1e-6
NEG_INF = -1e30
N_MIXERS = 2
LOG2E = 1.4426950408889634
LN2 = 0.6931471805599453

LANES = 128
VMEM_LIMIT = 58 * 1024 * 1024

F32 = jnp.float32
BF16 = jnp.bfloat16


def _params(sem, vmem=VMEM_LIMIT):
    return pltpu.CompilerParams(dimension_semantics=sem, vmem_limit_bytes=vmem)


def _ada_block(c_ref, w_ref, b_ref):
    return jnp.dot(c_ref[...].astype(BF16), w_ref[...].astype(BF16), preferred_element_type=F32) + b_ref[...]


def _ada_kernel(c_ref, w_ref, b_ref, o_ref):
    o_ref[...] = _ada_block(c_ref, w_ref, b_ref)


def ada_modulation(c_pad, w_ada, b_ada3, layer, bn=1024):
    _, d, n = w_ada.shape
    rows = c_pad.shape[0]
    return pl.pallas_call(
        _ada_kernel,
        out_shape=jax.ShapeDtypeStruct((rows, n), F32),
        grid=(n // bn,),
        in_specs=[
            pl.BlockSpec((rows, d), lambda j: (0, 0)),
            pl.BlockSpec((None, d, bn), lambda j: (layer, 0, j)),
            pl.BlockSpec((None, 1, bn), lambda j: (layer, 0, j)),
        ],
        out_specs=pl.BlockSpec((rows, bn), lambda j: (0, j)),
        compiler_params=_params(("parallel",)),
        name="ada_modulation",
    )(c_pad, w_ada, b_ada3)


def _rope_table_kernel(pos_ref, cos_ref, sin_ref):
    half = HEAD_DIM // 2
    pos = pos_ref[...].astype(F32)
    lane = lax.broadcasted_iota(jnp.int32, (1, HEAD_DIM), 1)
    fidx = (lane & (half - 1)).astype(F32)
    inv_freq = jnp.exp(-math.log(ROPE_THETA) * fidx / half)
    ang = pos * inv_freq
    cos_ref[...] = jnp.cos(ang)
    s = jnp.sin(ang)
    sin_ref[...] = jnp.where(lane < half, -s, s)


def rope_tables(pos_col, bm=1024):
    t = pos_col.shape[0]
    out = jax.ShapeDtypeStruct((t, HEAD_DIM), F32)
    return pl.pallas_call(
        _rope_table_kernel,
        out_shape=(out, out),
        grid=(t // bm,),
        in_specs=[pl.BlockSpec((bm, 1), lambda i: (i, 0))],
        out_specs=(pl.BlockSpec((bm, HEAD_DIM), lambda i: (i, 0)),
                   pl.BlockSpec((bm, HEAD_DIM), lambda i: (i, 0))),
        compiler_params=_params(("parallel",)),
        name="rope_tables",
    )(pos_col)


def _rms(x, g):
    ms = jnp.mean(x * x, axis=-1, keepdims=True)
    return x * lax.rsqrt(ms + RMS_EPS) * g


def _prenorm_kernel(x_ref, g_ref, sc_ref, sh_ref, h_ref):
    y = _rms(x_ref[...], g_ref[...])
    h_ref[...] = (y * (1.0 + sc_ref[...]) + sh_ref[...]).astype(h_ref.dtype)


def _mod_spec(chunk, d):
    return pl.BlockSpec((None, 1, d), lambda b, i: (b, 0, chunk))


def _vec_spec(layer, d):
    return pl.BlockSpec((None, 1, d), lambda b, i: (layer, 0, 0))


NORM_ROWS = 512


def prenorm(x, g3, layer, mod, sc_chunk, sh_chunk, bm=NORM_ROWS):
    nb, s, d = x.shape
    row = pl.BlockSpec((None, bm, d), lambda b, i: (b, i, 0))
    return pl.pallas_call(
        _prenorm_kernel,
        out_shape=jax.ShapeDtypeStruct(x.shape, BF16),
        grid=(nb, s // bm),
        in_specs=[row, _vec_spec(layer, d), _mod_spec(sc_chunk, d), _mod_spec(sh_chunk, d)],
        out_specs=row,
        compiler_params=_params(("parallel", "parallel")),
        name="prenorm",
    )(x, g3, mod, mod)


def _post_kernel(x_ref, y_ref, gpost_ref, gate_ref, xo_ref):
    yn = _rms(y_ref[...].astype(F32), gpost_ref[...])
    xo_ref[...] = x_ref[...] + gate_ref[...] * yn


def _post_next_kernel(x_ref, y_ref, gpost_ref, gate_ref, gpre_ref, sc_ref, sh_ref, xo_ref, h_ref):
    yn = _rms(y_ref[...].astype(F32), gpost_ref[...])
    xn = x_ref[...] + gate_ref[...] * yn
    xo_ref[...] = xn
    hn = _rms(xn, gpre_ref[...])
    h_ref[...] = (hn * (1.0 + sc_ref[...]) + sh_ref[...]).astype(h_ref.dtype)


def post_residual(x, y, gpost3, layer, mod, gate_chunk, nxt=None, bm=NORM_ROWS):
    nb, s, d = x.shape
    row = pl.BlockSpec((None, bm, d), lambda b, i: (b, i, 0))
    in_specs = [row, row, _vec_spec(layer, d), _mod_spec(gate_chunk, d)]
    args = [x, y, gpost3, mod]
    if nxt is None:
        return pl.pallas_call(
            _post_kernel,
            out_shape=jax.ShapeDtypeStruct(x.shape, F32),
            grid=(nb, s // bm),
            in_specs=in_specs,
            out_specs=row,
            compiler_params=_params(("parallel", "parallel")),
            name="post_residual",
        )(*args), None
    g3n, ln, modn, scn, shn = nxt
    in_specs += [_vec_spec(ln, d), _mod_spec(scn, d), _mod_spec(shn, d)]
    args += [g3n, modn, modn]
    return pl.pallas_call(
        _post_next_kernel,
        out_shape=(jax.ShapeDtypeStruct(x.shape, F32), jax.ShapeDtypeStruct(x.shape, BF16)),
        grid=(nb, s // bm),
        in_specs=in_specs,
        out_specs=(row, row),
        compiler_params=_params(("parallel", "parallel")),
        name="post_residual_next",
    )(*args)


def _act_spec(bm, k):
    return pl.BlockSpec((bm, k), lambda i, j: (i, 0), pipeline_mode=pl.Buffered(1))


def _mm_kernel(a_ref, w_ref, o_ref):
    w = w_ref[...].astype(BF16)
    o_ref[...] = jnp.dot(a_ref[...], w, preferred_element_type=F32).astype(o_ref.dtype)


def matmul_f32w(a, w3, layer, bm, bn, out_dtype=BF16, name="matmul"):
    m, k = a.shape
    n = w3.shape[2]
    return pl.pallas_call(
        _mm_kernel,
        out_shape=jax.ShapeDtypeStruct((m, n), out_dtype),
        grid=(m // bm, n // bn),
        in_specs=[_act_spec(bm, k),
                  pl.BlockSpec((None, k, bn), lambda i, j: (layer, 0, j))],
        out_specs=pl.BlockSpec((bm, bn), lambda i, j: (i, j)),
        compiler_params=_params(("arbitrary", "arbitrary")),
        name=name,
    )(a, w3)


def matmul_bf16w(a, w2, bm, bn, out_dtype=BF16, name="matmul"):
    m, k = a.shape
    n = w2.shape[1]
    return pl.pallas_call(
        _mm_kernel,
        out_shape=jax.ShapeDtypeStruct((m, n), out_dtype),
        grid=(m // bm, n // bn),
        in_specs=[pl.BlockSpec((bm, k), lambda i, j: (i, 0)),
                  pl.BlockSpec((k, bn), lambda i, j: (0, j))],
        out_specs=pl.BlockSpec((bm, bn), lambda i, j: (i, j)),
        compiler_params=_params(("arbitrary", "arbitrary")),
        name=name,
    )(a, w2)


def _qkv_kernel(a_ref, w_ref, cos_ref, sin_ref, o_ref, *, n_q_blocks, n_rope_blocks, q_scale):
    j = pl.program_id(1)
    w = w_ref[...].astype(BF16)
    is_rope = j < n_rope_blocks
    mult = jnp.where(j < n_q_blocks, q_scale, 1.0).astype(F32)
    n_heads = w.shape[1] // HEAD_DIM
    part = a_ref.shape[0] // 2
    for r in range(2):
        rows = slice(r * part, (r + 1) * part)
        acc = jnp.dot(a_ref[rows, :], w, preferred_element_type=F32)
        cos = jnp.where(is_rope, cos_ref[rows, :] * mult, 1.0)
        sin = jnp.where(is_rope, sin_ref[rows, :] * mult, 0.0)
        for h in range(n_heads):
            t = acc[:, h * HEAD_DIM:(h + 1) * HEAD_DIM]
            rot = pltpu.roll(t, HEAD_DIM // 2, 1)
            o_ref[rows, h * HEAD_DIM:(h + 1) * HEAD_DIM] = (t * cos + rot * sin).astype(o_ref.dtype)


def qkv_projection(h2, w3, layer, cos, sin, q_dim, kv_dim, bm=2048, bn=512):
    m, k = h2.shape
    n = w3.shape[2]
    kern = functools.partial(_qkv_kernel, n_q_blocks=q_dim // bn, n_rope_blocks=(q_dim + kv_dim) // bn,
                             q_scale=LOG2E / math.sqrt(HEAD_DIM))
    tab = pl.BlockSpec((bm, HEAD_DIM), lambda i, j: (i, 0))
    return pl.pallas_call(
        kern,
        out_shape=jax.ShapeDtypeStruct((m, n), BF16),
        grid=(m // bm, n // bn),
        in_specs=[_act_spec(bm, k),
                  pl.BlockSpec((None, k, bn), lambda i, j: (layer, 0, j)),
                  tab, tab],
        out_specs=pl.BlockSpec((bm, bn), lambda i, j: (i, j)),
        compiler_params=_params(("arbitrary", "arbitrary")),
        name="qkv_projection",
    )(h2, w3, cos, sin)


ADA_SIDE_BN = 256


def _gate_up_kernel(a_ref, wg_ref, wu_ref, wd_ref, *rest, with_ada):
    if with_ada:
        c_ref, wa_ref, ba_ref, o_ref, wdo_ref, mod_ref = rest
    else:
        o_ref, wdo_ref = rest
    wg = wg_ref[...].astype(BF16)
    wu = wu_ref[...].astype(BF16)
    part = a_ref.shape[0] // 2
    for r in range(2):
        rows = slice(r * part, (r + 1) * part)
        a = a_ref[rows, :]
        g = jnp.dot(a, wg, preferred_element_type=F32)
        u = jnp.dot(a, wu, preferred_element_type=F32)
        o_ref[rows, :] = (g * jax.nn.sigmoid(g) * u).astype(o_ref.dtype)
    wdo_ref[...] = wd_ref[...].astype(wdo_ref.dtype)
    if with_ada:
        mod_ref[...] = _ada_block(c_ref, wa_ref, ba_ref)


def gate_up(h2, w3, w_down3, layer, d_ff, ada_next=None, bm=2048, bn=256):
    m, k = h2.shape
    d_out = w_down3.shape[2]
    nblk = d_ff // bn
    steps = (m // bm) * nblk
    slab = d_ff // steps
    assert slab * steps == d_ff and slab % 16 == 0
    step = lambda i, j: i * nblk + j
    in_specs = [_act_spec(bm, k),
                pl.BlockSpec((None, k, bn), lambda i, j: (layer, 0, j)),
                pl.BlockSpec((None, k, bn), lambda i, j: (layer, 0, j + nblk)),
                pl.BlockSpec((None, slab, d_out), lambda i, j: (layer, step(i, j), 0))]
    out_shape = [jax.ShapeDtypeStruct((m, d_ff), BF16), jax.ShapeDtypeStruct((d_ff, d_out), BF16)]
    out_specs = [pl.BlockSpec((bm, bn), lambda i, j: (i, j)),
                 pl.BlockSpec((slab, d_out), lambda i, j: (step(i, j), 0))]
    args = [h2, w3, w3, w_down3]
    if ada_next is not None:
        c_pad, w_ada, b_ada3 = ada_next
        rows, n_mod = c_pad.shape[0], w_ada.shape[2]
        n_ada = n_mod // ADA_SIDE_BN
        assert n_ada <= steps
        ablk = lambda i, j: jnp.minimum(step(i, j), n_ada - 1)
        in_specs += [pl.BlockSpec((rows, k), lambda i, j: (0, 0)),
                     pl.BlockSpec((None, k, ADA_SIDE_BN), lambda i, j: (layer + 1, 0, ablk(i, j))),
                     pl.BlockSpec((None, 1, ADA_SIDE_BN), lambda i, j: (layer + 1, 0, ablk(i, j)))]
        out_shape.append(jax.ShapeDtypeStruct((rows, n_mod), F32))
        out_specs.append(pl.BlockSpec((rows, ADA_SIDE_BN), lambda i, j: (0, ablk(i, j))))
        args += [c_pad, w_ada, b_ada3]
    return pl.pallas_call(
        functools.partial(_gate_up_kernel, with_ada=ada_next is not None),
        out_shape=tuple(out_shape),
        grid=(m // bm, nblk),
        in_specs=in_specs,
        out_specs=tuple(out_specs),
        compiler_params=_params(("arbitrary", "arbitrary")),
        name="gate_up",
    )(*args)


def _scores_t(k, qs):
    return lax.dot_general(k, qs, (((1,), (1,)), ((), ())), preferred_element_type=F32)


def _head_rows(q):
    return jnp.concatenate([q[:, g * HEAD_DIM:(g + 1) * HEAD_DIM] for g in range(GROUP)], axis=0)


ATT_TQ = LANES
ATT_UNROLL = 16
ATT_SLOTS = 2


def _fill_band_masks(mask_ref, half_w):
    _, tk, tq = mask_ref.shape
    rel = (lax.broadcasted_iota(jnp.int32, (tk, tq), 1)
           - lax.broadcasted_iota(jnp.int32, (tk, tq), 0))
    for v in range(3):
        dist = rel + v * half_w
        mask_ref[v] = ((dist >= -half_w) & (dist <= half_w)).astype(jnp.int32)


def _window_start(q0, half_w, tk, seq_len):
    start = pl.multiple_of(jnp.clip(q0 - half_w, 0, seq_len - tk), 16)
    return start, (q0 - start) // half_w


def _tile_scores(q, k, st_buf):
    st_buf[...] = _scores_t(k, _head_rows(q))


def _tile_finish(v, mask, st_buf, p_buf, sinks=None):
    tq = mask.shape[1]
    keep = mask != 0
    dens, lses = [], []
    for g in range(GROUP):
        cols = slice(g * tq, (g + 1) * tq)
        s = jnp.where(keep, st_buf[:, cols], NEG_INF)
        m = jnp.max(s, axis=0, keepdims=True)
        if sinks is not None:
            m = jnp.maximum(m, sinks[g])
        p = jnp.exp2(s - m)
        den = jnp.sum(p, axis=0, keepdims=True)
        if sinks is not None:
            den = den + jnp.exp2(sinks[g] - m)
        p_buf[:, cols] = p.astype(BF16)
        dens.append(den)
        lses.append(m + jnp.log2(den))
    den = jnp.concatenate(dens, axis=1)
    ot = jnp.dot(v.T, p_buf[...], preferred_element_type=F32) / den
    return ot, jnp.concatenate(lses, axis=1)


def _run_tiles(n_tiles, scores, finish):
    assert n_tiles % ATT_UNROLL == 0 and ATT_UNROLL % ATT_SLOTS == 0
    scores(0, 0)

    def body(it, carry):
        for u in range(ATT_UNROLL):
            t = it * ATT_UNROLL + u
            scores(jnp.minimum(t + 1, n_tiles - 1), (u + 1) % ATT_SLOTS)
            finish(t, u % ATT_SLOTS)
        return carry

    lax.fori_loop(0, n_tiles // ATT_UNROLL, body, 0)


def _window_sink_kernel(sink_ref, q_ref, k_ref, v_ref, o_ref, mask_ref, st_buf, p_buf,
                        *, half_w, tk, seq_len):
    h = pl.program_id(1)
    tq = ATT_TQ
    _fill_band_masks(mask_ref, half_w)
    sinks = [sink_ref[0, h * GROUP + g] * LOG2E for g in range(GROUP)]

    def scores(t, slot):
        q0 = pl.multiple_of(t * tq, tq)
        start, _ = _window_start(q0, half_w, tk, seq_len)
        _tile_scores(q_ref[pl.ds(q0, tq), :], k_ref[pl.ds(start, tk), :], st_buf.at[slot])

    def finish(t, slot):
        q0 = pl.multiple_of(t * tq, tq)
        start, variant = _window_start(q0, half_w, tk, seq_len)
        ot, _ = _tile_finish(v_ref[pl.ds(start, tk), :], mask_ref[variant],
                             st_buf.at[slot], p_buf.at[slot], sinks)
        for g in range(GROUP):
            o_ref[pl.ds(q0, tq), g * HEAD_DIM:(g + 1) * HEAD_DIM] = (
                ot[:, g * tq:(g + 1) * tq].T.astype(o_ref.dtype))

    _run_tiles(seq_len // tq, scores, finish)


def window_sink_attention(qkv, half_w, n_kv_heads, sink):
    nb, seq_len, _ = qkv.shape
    tq = ATT_TQ
    tk = tq + 2 * half_w
    assert tq % half_w == 0 and (seq_len - tk) % half_w == 0 and seq_len % (tq * ATT_UNROLL) == 0
    n_q_cols = n_kv_heads * GROUP
    kern = functools.partial(_window_sink_kernel, half_w=half_w, tk=tk, seq_len=seq_len)
    return pl.pallas_call(
        kern,
        out_shape=jax.ShapeDtypeStruct((nb, seq_len, n_q_cols * HEAD_DIM), BF16),
        grid=(nb, n_kv_heads),
        in_specs=[
            pl.BlockSpec(memory_space=pltpu.SMEM),
            pl.BlockSpec((None, seq_len, GROUP * HEAD_DIM), lambda b, h: (b, 0, h)),
            pl.BlockSpec((None, seq_len, HEAD_DIM), lambda b, h: (b, 0, n_q_cols + h)),
            pl.BlockSpec((None, seq_len, HEAD_DIM), lambda b, h: (b, 0, n_q_cols + n_kv_heads + h)),
        ],
        out_specs=pl.BlockSpec((None, seq_len, GROUP * HEAD_DIM), lambda b, h: (b, 0, h)),
        scratch_shapes=[pltpu.VMEM((3, tk, tq), jnp.int32),
                        pltpu.VMEM((ATT_SLOTS, tk, GROUP * tq), F32),
                        pltpu.VMEM((ATT_SLOTS, tk, GROUP * tq), BF16)],
        compiler_params=_params(("arbitrary", "arbitrary")),
        name="window_sink_attention",
    )(sink.reshape(1, -1).astype(F32), qkv, qkv, qkv)


SUPER = 1024


def _dilate_kernel(x_ref, r4_ref, r16_ref, slab, slab4):
    for c in range(x_ref.shape[1] // LANES):
        cols = slice(c * LANES, (c + 1) * LANES)
        slab[...] = x_ref[:, cols].astype(F32)
        for res in range(4):
            cls = slab[pl.ds(res, SUPER // 4, stride=4), :]
            r4_ref[res, :, cols] = cls.astype(r4_ref.dtype)
            slab4[res] = cls
        for res in range(16):
            cls = slab4[res % 4, pl.ds(res // 4, SUPER // 16, stride=4), :]
            r16_ref[res, :, cols] = cls.astype(r16_ref.dtype)


def dilate(qkv, bw=1536):
    nb, s, c = qkv.shape
    return pl.pallas_call(
        _dilate_kernel,
        out_shape=(jax.ShapeDtypeStruct((nb, 4, s // 4, c), BF16),
                   jax.ShapeDtypeStruct((nb, 16, s // 16, c), BF16)),
        grid=(nb, s // SUPER, c // bw),
        in_specs=[pl.BlockSpec((None, SUPER, bw), lambda b, i, j: (b, i, j))],
        out_specs=(pl.BlockSpec((None, 4, SUPER // 4, bw), lambda b, i, j: (b, 0, i, j)),
                   pl.BlockSpec((None, 16, SUPER // 16, bw), lambda b, i, j: (b, 0, i, j))),
        scratch_shapes=[pltpu.VMEM((SUPER, LANES), F32), pltpu.VMEM((4, SUPER // 4, LANES), F32)],
        compiler_params=_params(("parallel", "parallel", "parallel")),
        name="dilate",
    )(qkv)


DIL_HALF = 64
DIL_SUPER = 2048


def _dilated_kernel(q1_ref, k1_ref, v1_ref, q4_ref, k4_ref, v4_ref, q16_ref, k16_ref, v16_ref,
                    out_ref, mask_ref, st_buf, p_buf, o_scr, st1, st4, st16, *, seq_len):
    n = pl.program_id(2)
    tq, w = ATT_TQ, DIL_HALF
    tk = tq + 2 * w
    n_tiles = DIL_SUPER // tq
    _fill_band_masks(mask_ref, w)

    def stat_tile(lse2):
        rows = [lse2[:, g * tq:(g + 1) * tq] * LN2 for g in range(GROUP)]
        rows.append(jnp.zeros((LANES - GROUP, tq), F32))
        return jnp.concatenate(rows, axis=0).T

    def branch(seq, geometry, load_q, load_kv, out_rows, base, stat_ref):
        def scores(t, slot):
            q0, hnd = geometry(t)
            start, _ = _window_start(q0, w, tk, seq)
            _tile_scores(load_q(hnd), load_kv(0, hnd, start), st_buf.at[slot])

        def finish(t, slot):
            q0, hnd = geometry(t)
            start, variant = _window_start(q0, w, tk, seq)
            ot, lse2 = _tile_finish(load_kv(1, hnd, start), mask_ref[variant],
                                    st_buf.at[slot], p_buf.at[slot])
            rows = out_rows(hnd)
            for g in range(GROUP):
                o_scr[base + g, rows, :] = ot[:, g * tq:(g + 1) * tq].T
            stat_ref[rows, :] = stat_tile(lse2)

        _run_tiles(n_tiles, scores, finish)

    kv1 = (k1_ref, v1_ref)
    branch(seq_len,
           lambda t: (n * DIL_SUPER + t * tq, pl.multiple_of(t * tq, tq)),
           lambda r0: q1_ref[pl.ds(r0, tq), :],
           lambda i, r0, start: kv1[i][pl.ds(start, tk), :],
           lambda r0: pl.ds(r0, tq),
           0, st1)

    per4 = DIL_SUPER // 4
    kv4 = (k4_ref, v4_ref)
    branch(seq_len // 4,
           lambda t: (n * per4 + (t & 3) * tq, (t >> 2, pl.multiple_of((t & 3) * tq, tq))),
           lambda h: q4_ref[h[0], pl.ds(h[1], tq), :],
           lambda i, h, start: kv4[i][h[0], pl.ds(start, tk), :],
           lambda h: pl.ds(4 * h[1] + h[0], tq, stride=4),
           GROUP, st4)

    assert DIL_SUPER // 16 == tq
    kv16 = (k16_ref, v16_ref)
    branch(seq_len // 16,
           lambda t: (n * tq, t),
           lambda c: q16_ref[c],
           lambda i, c, start: kv16[i][c, pl.ds(start, tk), :],
           lambda c: pl.ds(c, tq, stride=16),
           2 * GROUP, st16)

    a1, a4, a16 = st1[...], st4[...], st16[...]
    mx = jnp.maximum(jnp.maximum(a1, a4), a16)
    e1 = jnp.exp(a1 - mx)
    e4 = jnp.exp(a4 - mx)
    inv = 1.0 / (e1 + e4 + jnp.exp(a16 - mx))
    alpha1 = e1 * inv
    alpha4 = e4 * inv
    for g in range(GROUP):
        o16 = o_scr[2 * GROUP + g]
        o = (o16 + alpha1[:, g:g + 1] * (o_scr[g] - o16)
             + alpha4[:, g:g + 1] * (o_scr[GROUP + g] - o16))
        out_ref[:, g * HEAD_DIM:(g + 1) * HEAD_DIM] = o.astype(out_ref.dtype)


def dilated_attention(nat, r4, r16, n_kv_heads):
    nb, s, c = nat.shape
    gw = GROUP * HEAD_DIM
    nqc = n_kv_heads * GROUP
    kcol = lambda h: nqc + h
    vcol = lambda h: nqc + n_kv_heads + h
    sup, tq = DIL_SUPER, ATT_TQ
    tk = tq + 2 * DIL_HALF
    assert s % sup == 0 and s // 16 >= tk and tq % DIL_HALF == 0
    in_specs = [
        pl.BlockSpec((None, sup, gw), lambda b, h, n: (b, n, h)),
        pl.BlockSpec((None, s, HEAD_DIM), lambda b, h, n: (b, 0, kcol(h))),
        pl.BlockSpec((None, s, HEAD_DIM), lambda b, h, n: (b, 0, vcol(h))),
        pl.BlockSpec((None, 4, sup // 4, gw), lambda b, h, n: (b, 0, n, h)),
        pl.BlockSpec((None, 4, s // 4, HEAD_DIM), lambda b, h, n: (b, 0, 0, kcol(h))),
        pl.BlockSpec((None, 4, s // 4, HEAD_DIM), lambda b, h, n: (b, 0, 0, vcol(h))),
        pl.BlockSpec((None, 16, sup // 16, gw), lambda b, h, n: (b, 0, n, h)),
        pl.BlockSpec((None, 16, s // 16, HEAD_DIM), lambda b, h, n: (b, 0, 0, kcol(h))),
        pl.BlockSpec((None, 16, s // 16, HEAD_DIM), lambda b, h, n: (b, 0, 0, vcol(h))),
    ]
    return pl.pallas_call(
        functools.partial(_dilated_kernel, seq_len=s),
        out_shape=jax.ShapeDtypeStruct((nb, s, nqc * HEAD_DIM), BF16),
        grid=(nb, n_kv_heads, s // sup),
        in_specs=in_specs,
        out_specs=pl.BlockSpec((None, sup, gw), lambda b, h, n: (b, n, h)),
        scratch_shapes=[pltpu.VMEM((3, tk, tq), jnp.int32),
                        pltpu.VMEM((ATT_SLOTS, tk, GROUP * tq), F32),
                        pltpu.VMEM((ATT_SLOTS, tk, GROUP * tq), BF16),
                        pltpu.VMEM((3 * GROUP, sup, LANES), F32),
                        pltpu.VMEM((sup, LANES), F32),
                        pltpu.VMEM((sup, LANES), F32),
                        pltpu.VMEM((sup, LANES), F32)],
        compiler_params=_params(("arbitrary", "arbitrary", "arbitrary")),
        name="dilated_attention",
    )(nat, nat, nat, r4, r4, r4, r16, r16, r16)


def kernel(x, c, positions, w_qkv, w_o, sink, w_gate_up, w_down, g_mix_pre, g_mix_post,
           g_ffn_pre, g_ffn_post, w_ada, b_ada):
    nb, s, d = x.shape
    depth = w_qkv.shape[0]
    t = nb * s
    q_dim = w_o.shape[1]
    kv_dim = (w_qkv.shape[2] - q_dim) // 2
    n_kv_heads = kv_dim // HEAD_DIM
    d_ff = w_down.shape[1]
    assert [r for _, r in DILATED_BRANCHES] == [1, 4, 16] and all(w // (2 * r) == 64 for w, r in DILATED_BRANCHES)

    c_pad = jnp.pad(c, ((0, 8 - nb), (0, 0)))
    b_ada3 = b_ada.reshape(depth, 1, 6 * d)
    as_mod = lambda m2: m2.reshape(8, 1, 6 * d)
    mod = as_mod(ada_modulation(c_pad, w_ada, b_ada3, 0))
    cos, sin = rope_tables(positions.reshape(t, 1))

    g_mix_pre3 = g_mix_pre.reshape(depth, 1, d)
    g_mix_post3 = g_mix_post.reshape(depth, 1, d)
    g_ffn_pre3 = g_ffn_pre.reshape(depth, 1, d)
    g_ffn_post3 = g_ffn_post.reshape(depth, 1, d)
    SH_T, SC_T, GT_T, SH_F, SC_F, GT_F = range(6)

    h = prenorm(x, g_mix_pre3, 0, mod, SC_T, SH_T)
    for i in range(depth):
        last = i + 1 == depth
        h2 = h.reshape(t, d)
        qkv = qkv_projection(h2, w_qkv, i, cos, sin, q_dim, kv_dim).reshape(nb, s, -1)
        if i % N_MIXERS == 0:
            att = window_sink_attention(qkv, WINDOW_A_HALF, n_kv_heads, sink[i // N_MIXERS])
        else:
            r4, r16 = dilate(qkv)
            att = dilated_attention(qkv, r4, r16, n_kv_heads)
        y = matmul_f32w(att.reshape(t, q_dim), w_o, i, bm=2048, bn=512, name="wo_projection")
        x, h = post_residual(x, y.reshape(nb, s, d), g_mix_post3, i, mod, GT_T,
                             nxt=(g_ffn_pre3, i, mod, SC_F, SH_F))
        outs = gate_up(h.reshape(t, d), w_gate_up, w_down, i, d_ff,
                       ada_next=None if last else (c_pad, w_ada, b_ada3))
        a, wd_b = outs[:2]
        y = matmul_bf16w(a, wd_b, bm=512, bn=512, name="down_projection")
        mod_next = None if last else as_mod(outs[2])
        nxt = None if last else (g_mix_pre3, i + 1, mod_next, SC_T, SH_T)
        x, h = post_residual(x, y.reshape(nb, s, d), g_ffn_post3, i, mod, GT_F, nxt=nxt)
        mod = mod_next
    return x
```

```python
import functools
import math

import jax
import jax.numpy as jnp
from jax import lax
from jax.experimental import pallas as pl
from jax.experimental.pallas import tpu as pltpu

HEAD_DIM = 128
GROUP = 4
ROPE_THETA = 10000.0
WINDOW_A_HALF = 128
DILATED_BRANCHES = ((128, 1), (512, 4), (2048, 16))
RMS_EPS = 1e-6
NEG_INF = -1e30
N_MIXERS = 2
LOG2E = 1.4426950408889634
LN2 = 0.6931471805599453

LANES = 128
VMEM_LIMIT = 58 * 1024 * 1024

F32 = jnp.float32
BF16 = jnp.bfloat16


def _params(sem, vmem=VMEM_LIMIT):
    return pltpu.CompilerParams(dimension_semantics=sem, vmem_limit_bytes=vmem)


def _ada_block(c_ref, w_ref, b_ref):
    return jnp.dot(c_ref[...].astype(BF16), w_ref[...].astype(BF16), preferred_element_type=F32) + b_ref[...]


def _ada_kernel(c_ref, w_ref, b_ref, o_ref):
    o_ref[...] = _ada_block(c_ref, w_ref, b_ref)


def ada_modulation(c_pad, w_ada, b_ada3, layer, bn=1024):
    _, d, n = w_ada.shape
    rows = c_pad.shape[0]
    return pl.pallas_call(
        _ada_kernel,
        out_shape=jax.ShapeDtypeStruct((rows, n), F32),
        grid=(n // bn,),
        in_specs=[
            pl.BlockSpec((rows, d), lambda j: (0, 0)),
            pl.BlockSpec((None, d, bn), lambda j: (layer, 0, j)),
            pl.BlockSpec((None, 1, bn), lambda j: (layer, 0, j)),
        ],
        out_specs=pl.BlockSpec((rows, bn), lambda j: (0, j)),
        compiler_params=_params(("parallel",)),
        name="ada_modulation",
    )(c_pad, w_ada, b_ada3)


def _rope_table_kernel(pos_ref, cos_ref, sin_ref):
    half = HEAD_DIM // 2
    pos = pos_ref[...].astype(F32)
    lane = lax.broadcasted_iota(jnp.int32, (1, HEAD_DIM), 1)
    fidx = (lane & (half - 1)).astype(F32)
    inv_freq = jnp.exp(-math.log(ROPE_THETA) * fidx / half)
    ang = pos * inv_freq
    cos_ref[...] = jnp.cos(ang)
    s = jnp.sin(ang)
    sin_ref[...] = jnp.where(lane < half, -s, s)


def rope_tables(pos_col, bm=1024):
    t = pos_col.shape[0]
    out = jax.ShapeDtypeStruct((t, HEAD_DIM), F32)
    return pl.pallas_call(
        _rope_table_kernel,
        out_shape=(out, out),
        grid=(t // bm,),
        in_specs=[pl.BlockSpec((bm, 1), lambda i: (i, 0))],
        out_specs=(pl.BlockSpec((bm, HEAD_DIM), lambda i: (i, 0)),
                   pl.BlockSpec((bm, HEAD_DIM), lambda i: (i, 0))),
        compiler_params=_params(("parallel",)),
        name="rope_tables",
    )(pos_col)


def _rms(x, g):
    ms = jnp.mean(x * x, axis=-1, keepdims=True)
    return x * lax.rsqrt(ms + RMS_EPS) * g


def _prenorm_kernel(x_ref, g_ref, sc_ref, sh_ref, h_ref):
    y = _rms(x_ref[...], g_ref[...])
    h_ref[...] = (y * (1.0 + sc_ref[...]) + sh_ref[...]).astype(h_ref.dtype)


def _mod_spec(chunk, d):
    return pl.BlockSpec((None, 1, d), lambda b, i: (b, 0, chunk))


def _vec_spec(layer, d):
    return pl.BlockSpec((None, 1, d), lambda b, i: (layer, 0, 0))


NORM_ROWS = 512


def prenorm(x, g3, layer, mod, sc_chunk, sh_chunk, bm=NORM_ROWS):
    nb, s, d = x.shape
    row = pl.BlockSpec((None, bm, d), lambda b, i: (b, i, 0))
    return pl.pallas_call(
        _prenorm_kernel,
        out_shape=jax.ShapeDtypeStruct(x.shape, BF16),
        grid=(nb, s // bm),
        in_specs=[row, _vec_spec(layer, d), _mod_spec(sc_chunk, d), _mod_spec(sh_chunk, d)],
        out_specs=row,
        compiler_params=_params(("parallel", "parallel")),
        name="prenorm",
    )(x, g3, mod, mod)


def _post_kernel(x_ref, y_ref, gpost_ref, gate_ref, xo_ref):
    yn = _rms(y_ref[...].astype(F32), gpost_ref[...])
    xo_ref[...] = x_ref[...] + gate_ref[...] * yn


def _post_next_kernel(x_ref, y_ref, gpost_ref, gate_ref, gpre_ref, sc_ref, sh_ref, xo_ref, h_ref):
    yn = _rms(y_ref[...].astype(F32), gpost_ref[...])
    xn = x_ref[...] + gate_ref[...] * yn
    xo_ref[...] = xn
    hn = _rms(xn, gpre_ref[...])
    h_ref[...] = (hn * (1.0 + sc_ref[...]) + sh_ref[...]).astype(h_ref.dtype)


def post_residual(x, y, gpost3, layer, mod, gate_chunk, nxt=None, bm=NORM_ROWS):
    nb, s, d = x.shape
    row = pl.BlockSpec((None, bm, d), lambda b, i: (b, i, 0))
    in_specs = [row, row, _vec_spec(layer, d), _mod_spec(gate_chunk, d)]
    args = [x, y, gpost3, mod]
    if nxt is None:
        return pl.pallas_call(
            _post_kernel,
            out_shape=jax.ShapeDtypeStruct(x.shape, F32),
            grid=(nb, s // bm),
            in_specs=in_specs,
            out_specs=row,
            compiler_params=_params(("parallel", "parallel")),
            name="post_residual",
        )(*args), None
    g3n, ln, modn, scn, shn = nxt
    in_specs += [_vec_spec(ln, d), _mod_spec(scn, d), _mod_spec(shn, d)]
    args += [g3n, modn, modn]
    return pl.pallas_call(
        _post_next_kernel,
        out_shape=(jax.ShapeDtypeStruct(x.shape, F32), jax.ShapeDtypeStruct(x.shape, BF16)),
        grid=(nb, s // bm),
        in_specs=in_specs,
        out_specs=(row, row),
        compiler_params=_params(("parallel", "parallel")),
        name="post_residual_next",
    )(*args)


def _act_spec(bm, k):
    return pl.BlockSpec((bm, k), lambda i, j: (i, 0), pipeline_mode=pl.Buffered(1))


ROW_PARTS = 4


def _mm_kernel(a_ref, w_ref, o_ref):
    w = w_ref[...]
    part = a_ref.shape[0] // ROW_PARTS
    for r in range(ROW_PARTS):
        rows = slice(r * part, (r + 1) * part)
        o_ref[rows, :] = jnp.dot(a_ref[rows, :], w, preferred_element_type=F32).astype(o_ref.dtype)


def matmul_bf16w(a, w2, bm, bn, out_dtype=BF16, name="matmul"):
    m, k = a.shape
    n = w2.shape[1]
    return pl.pallas_call(
        _mm_kernel,
        out_shape=jax.ShapeDtypeStruct((m, n), out_dtype),
        grid=(m // bm, n // bn),
        in_specs=[pl.BlockSpec((bm, k), lambda i, j: (i, 0)),
                  pl.BlockSpec((k, bn), lambda i, j: (0, j))],
        out_specs=pl.BlockSpec((bm, bn), lambda i, j: (i, j)),
        compiler_params=_params(("arbitrary", "arbitrary")),
        name=name,
    )(a, w2)


def _qkv_kernel(a_ref, w_ref, cos_ref, sin_ref, o_ref, *, n_q_blocks, n_rope_blocks, q_scale):
    j = pl.program_id(1)
    w = w_ref[...].astype(BF16)
    is_rope = j < n_rope_blocks
    mult = jnp.where(j < n_q_blocks, q_scale, 1.0).astype(F32)
    n_heads = w.shape[1] // HEAD_DIM
    part = a_ref.shape[0] // ROW_PARTS
    for r in range(ROW_PARTS):
        rows = slice(r * part, (r + 1) * part)
        acc = jnp.dot(a_ref[rows, :], w, preferred_element_type=F32)
        cos = jnp.where(is_rope, cos_ref[rows, :] * mult, 1.0)
        sin = jnp.where(is_rope, sin_ref[rows, :] * mult, 0.0)
        for h in range(n_heads):
            t = acc[:, h * HEAD_DIM:(h + 1) * HEAD_DIM]
            rot = pltpu.roll(t, HEAD_DIM // 2, 1)
            o_ref[rows, h * HEAD_DIM:(h + 1) * HEAD_DIM] = (t * cos + rot * sin).astype(o_ref.dtype)


def qkv_projection(h2, w3, layer, cos, sin, q_dim, kv_dim, bm=2048, bn=512):
    m, k = h2.shape
    n = w3.shape[2]
    kern = functools.partial(_qkv_kernel, n_q_blocks=q_dim // bn, n_rope_blocks=(q_dim + kv_dim) // bn,
                             q_scale=LOG2E / math.sqrt(HEAD_DIM))
    tab = pl.BlockSpec((bm, HEAD_DIM), lambda i, j: (i, 0))
    return pl.pallas_call(
        kern,
        out_shape=jax.ShapeDtypeStruct((m, n), BF16),
        grid=(m // bm, n // bn),
        in_specs=[_act_spec(bm, k),
                  pl.BlockSpec((None, k, bn), lambda i, j: (layer, 0, j)),
                  tab, tab],
        out_specs=pl.BlockSpec((bm, bn), lambda i, j: (i, j)),
        compiler_params=_params(("arbitrary", "arbitrary")),
        name="qkv_projection",
    )(h2, w3, cos, sin)


ADA_SIDE_BN = 256


def _gate_up_kernel(a_ref, wg_ref, wu_ref, wd_ref, *rest, with_ada):
    if with_ada:
        c_ref, wa_ref, ba_ref, o_ref, wdo_ref, mod_ref = rest
    else:
        o_ref, wdo_ref = rest
    wg = wg_ref[...].astype(BF16)
    wu = wu_ref[...].astype(BF16)
    part = a_ref.shape[0] // ROW_PARTS
    for r in range(ROW_PARTS):
        rows = slice(r * part, (r + 1) * part)
        a = a_ref[rows, :]
        g = jnp.dot(a, wg, preferred_element_type=F32)
        u = jnp.dot(a, wu, preferred_element_type=F32)
        o_ref[rows, :] = (g * jax.nn.sigmoid(g) * u).astype(o_ref.dtype)
    wdo_ref[...] = wd_ref[...].astype(wdo_ref.dtype)
    if with_ada:
        mod_ref[...] = _ada_block(c_ref, wa_ref, ba_ref)


def gate_up(h2, w3, w_down3, layer, d_ff, ada_next=None, bm=2048, bn=256):
    m, k = h2.shape
    d_out = w_down3.shape[2]
    nblk = d_ff // bn
    steps = (m // bm) * nblk
    slab = d_ff // steps
    assert slab * steps == d_ff and slab % 16 == 0
    step = lambda i, j: i * nblk + j
    in_specs = [_act_spec(bm, k),
                pl.BlockSpec((None, k, bn), lambda i, j: (layer, 0, j)),
                pl.BlockSpec((None, k, bn), lambda i, j: (layer, 0, j + nblk)),
                pl.BlockSpec((None, slab, d_out), lambda i, j: (layer, step(i, j), 0))]
    out_shape = [jax.ShapeDtypeStruct((m, d_ff), BF16), jax.ShapeDtypeStruct((d_ff, d_out), BF16)]
    out_specs = [pl.BlockSpec((bm, bn), lambda i, j: (i, j)),
                 pl.BlockSpec((slab, d_out), lambda i, j: (step(i, j), 0))]
    args = [h2, w3, w3, w_down3]
    if ada_next is not None:
        c_pad, w_ada, b_ada3 = ada_next
        rows, n_mod = c_pad.shape[0], w_ada.shape[2]
        n_ada = n_mod // ADA_SIDE_BN
        assert n_ada <= steps
        ablk = lambda i, j: jnp.minimum(step(i, j), n_ada - 1)
        in_specs += [pl.BlockSpec((rows, k), lambda i, j: (0, 0)),
                     pl.BlockSpec((None, k, ADA_SIDE_BN), lambda i, j: (layer + 1, 0, ablk(i, j))),
                     pl.BlockSpec((None, 1, ADA_SIDE_BN), lambda i, j: (layer + 1, 0, ablk(i, j)))]
        out_shape.append(jax.ShapeDtypeStruct((rows, n_mod), F32))
        out_specs.append(pl.BlockSpec((rows, ADA_SIDE_BN), lambda i, j: (0, ablk(i, j))))
        args += [c_pad, w_ada, b_ada3]
    return pl.pallas_call(
        functools.partial(_gate_up_kernel, with_ada=ada_next is not None),
        out_shape=tuple(out_shape),
        grid=(m // bm, nblk),
        in_specs=in_specs,
        out_specs=tuple(out_specs),
        compiler_params=_params(("arbitrary", "arbitrary")),
        name="gate_up",
    )(*args)


def _scores_t(k, qs):
    return lax.dot_general(k, qs, (((1,), (1,)), ((), ())), preferred_element_type=F32)


def _head_rows(q):
    return jnp.concatenate([q[:, g * HEAD_DIM:(g + 1) * HEAD_DIM] for g in range(GROUP)], axis=0)


ATT_TQ = LANES
ATT_UNROLL = 16
ATT_SLOTS = 2


def _fill_band_masks(mask_ref, half_w):
    _, tk, tq = mask_ref.shape
    rel = (lax.broadcasted_iota(jnp.int32, (tk, tq), 1)
           - lax.broadcasted_iota(jnp.int32, (tk, tq), 0))
    for v in range(3):
        dist = rel + v * half_w
        mask_ref[v] = ((dist >= -half_w) & (dist <= half_w)).astype(jnp.int32)


def _window_start(q0, half_w, tk, seq_len):
    start = pl.multiple_of(jnp.clip(q0 - half_w, 0, seq_len - tk), 16)
    return start, (q0 - start) // half_w


def _tile_scores(q, k, st_buf):
    st_buf[...] = _scores_t(k, _head_rows(q))


def _tile_finish(v, mask, st_buf, p_buf, sinks=None):
    tq = mask.shape[1]
    keep = mask != 0
    dens, lses = [], []
    for g in range(GROUP):
        cols = slice(g * tq, (g + 1) * tq)
        s = jnp.where(keep, st_buf[:, cols], NEG_INF)
        m = jnp.max(s, axis=0, keepdims=True)
        if sinks is not None:
            m = jnp.maximum(m, sinks[g])
        p = jnp.exp2(s - m)
        den = jnp.sum(p, axis=0, keepdims=True)
        if sinks is not None:
            den = den + jnp.exp2(sinks[g] - m)
        p_buf[:, cols] = p.astype(BF16)
        dens.append(den)
        lses.append(m + jnp.log2(den))
    den = jnp.concatenate(dens, axis=1)
    ot = jnp.dot(v.T, p_buf[...], preferred_element_type=F32) / den
    return ot, jnp.concatenate(lses, axis=1)


def _run_tiles(n_tiles, scores, finish):
    assert n_tiles % ATT_UNROLL == 0 and ATT_UNROLL % ATT_SLOTS == 0
    scores(0, 0)

    def body(it, carry):
        for u in range(ATT_UNROLL):
            t = it * ATT_UNROLL + u
            scores(jnp.minimum(t + 1, n_tiles - 1), (u + 1) % ATT_SLOTS)
            finish(t, u % ATT_SLOTS)
        return carry

    lax.fori_loop(0, n_tiles // ATT_UNROLL, body, 0)


def _wo_side_specs(w_o3, layer, n_steps, step_of):
    _, rows, cols = w_o3.shape
    slab = rows // n_steps
    assert slab * n_steps == rows and slab % 16 == 0
    return (pl.BlockSpec((None, slab, cols), lambda *g: (layer, step_of(*g), 0)),
            pl.BlockSpec((slab, cols), lambda *g: (step_of(*g), 0)),
            jax.ShapeDtypeStruct((rows, cols), BF16))


def _window_sink_kernel(sink_ref, q_ref, k_ref, v_ref, wo_ref, o_ref, wob_ref, mask_ref, st_buf, p_buf,
                        *, half_w, tk, seq_len):
    h = pl.program_id(1)
    tq = ATT_TQ
    wob_ref[...] = wo_ref[...].astype(wob_ref.dtype)
    _fill_band_masks(mask_ref, half_w)
    sinks = [sink_ref[0, h * GROUP + g] * LOG2E for g in range(GROUP)]

    def scores(t, slot):
        q0 = pl.multiple_of(t * tq, tq)
        start, _ = _window_start(q0, half_w, tk, seq_len)
        _tile_scores(q_ref[pl.ds(q0, tq), :], k_ref[pl.ds(start, tk), :], st_buf.at[slot])

    def finish(t, slot):
        q0 = pl.multiple_of(t * tq, tq)
        start, variant = _window_start(q0, half_w, tk, seq_len)
        ot, _ = _tile_finish(v_ref[pl.ds(start, tk), :], mask_ref[variant],
                             st_buf.at[slot], p_buf.at[slot], sinks)
        for g in range(GROUP):
            o_ref[pl.ds(q0, tq), g * HEAD_DIM:(g + 1) * HEAD_DIM] = (
                ot[:, g * tq:(g + 1) * tq].T.astype(o_ref.dtype))

    _run_tiles(seq_len // tq, scores, finish)


def window_sink_attention(qkv, half_w, n_kv_heads, sink, w_o3, layer):
    nb, seq_len, _ = qkv.shape
    tq = ATT_TQ
    tk = tq + 2 * half_w
    assert tq % half_w == 0 and (seq_len - tk) % half_w == 0 and seq_len % (tq * ATT_UNROLL) == 0
    n_q_cols = n_kv_heads * GROUP
    kern = functools.partial(_window_sink_kernel, half_w=half_w, tk=tk, seq_len=seq_len)
    wo_in, wo_out, wo_shape = _wo_side_specs(w_o3, layer, nb * n_kv_heads, lambda b, h: b * n_kv_heads + h)
    return pl.pallas_call(
        kern,
        out_shape=(jax.ShapeDtypeStruct((nb, seq_len, n_q_cols * HEAD_DIM), BF16), wo_shape),
        grid=(nb, n_kv_heads),
        in_specs=[
            pl.BlockSpec(memory_space=pltpu.SMEM),
            pl.BlockSpec((None, seq_len, GROUP * HEAD_DIM), lambda b, h: (b, 0, h)),
            pl.BlockSpec((None, seq_len, HEAD_DIM), lambda b, h: (b, 0, n_q_cols + h)),
            pl.BlockSpec((None, seq_len, HEAD_DIM), lambda b, h: (b, 0, n_q_cols + n_kv_heads + h)),
            wo_in,
        ],
        out_specs=(pl.BlockSpec((None, seq_len, GROUP * HEAD_DIM), lambda b, h: (b, 0, h)), wo_out),
        scratch_shapes=[pltpu.VMEM((3, tk, tq), jnp.int32),
                        pltpu.VMEM((ATT_SLOTS, tk, GROUP * tq), F32),
                        pltpu.VMEM((ATT_SLOTS, tk, GROUP * tq), BF16)],
        compiler_params=_params(("arbitrary", "arbitrary")),
        name="window_sink_attention",
    )(sink.reshape(1, -1).astype(F32), qkv, qkv, qkv, w_o3)


SUPER = 1024


def _dilate_kernel(x_ref, r4_ref, r16_ref, slab, slab4):
    for c in range(x_ref.shape[1] // LANES):
        cols = slice(c * LANES, (c + 1) * LANES)
        slab[...] = x_ref[:, cols].astype(F32)
        for res in range(4):
            cls = slab[pl.ds(res, SUPER // 4, stride=4), :]
            r4_ref[res, :, cols] = cls.astype(r4_ref.dtype)
            slab4[res] = cls
        for res in range(16):
            cls = slab4[res % 4, pl.ds(res // 4, SUPER // 16, stride=4), :]
            r16_ref[res, :, cols] = cls.astype(r16_ref.dtype)


def dilate(qkv, bw=1536):
    nb, s, c = qkv.shape
    return pl.pallas_call(
        _dilate_kernel,
        out_shape=(jax.ShapeDtypeStruct((nb, 4, s // 4, c), BF16),
                   jax.ShapeDtypeStruct((nb, 16, s // 16, c), BF16)),
        grid=(nb, s // SUPER, c // bw),
        in_specs=[pl.BlockSpec((None, SUPER, bw), lambda b, i, j: (b, i, j))],
        out_specs=(pl.BlockSpec((None, 4, SUPER // 4, bw), lambda b, i, j: (b, 0, i, j)),
                   pl.BlockSpec((None, 16, SUPER // 16, bw), lambda b, i, j: (b, 0, i, j))),
        scratch_shapes=[pltpu.VMEM((SUPER, LANES), F32), pltpu.VMEM((4, SUPER // 4, LANES), F32)],
        compiler_params=_params(("parallel", "parallel", "parallel")),
        name="dilate",
    )(qkv)


DIL_HALF = 64
DIL_SUPER = 2048


def _dilated_kernel(q1_ref, k1_ref, v1_ref, q4_ref, k4_ref, v4_ref, q16_ref, k16_ref, v16_ref, wo_ref,
                    out_ref, wob_ref, mask_ref, st_buf, p_buf, o_scr, st1, st4, st16, *, seq_len):
    n = pl.program_id(2)
    tq, w = ATT_TQ, DIL_HALF
    tk = tq + 2 * w
    n_tiles = DIL_SUPER // tq
    wob_ref[...] = wo_ref[...].astype(wob_ref.dtype)
    _fill_band_masks(mask_ref, w)

    def stat_tile(lse2):
        rows = [lse2[:, g * tq:(g + 1) * tq] * LN2 for g in range(GROUP)]
        rows.append(jnp.zeros((LANES - GROUP, tq), F32))
        return jnp.concatenate(rows, axis=0).T

    def branch(seq, geometry, load_q, load_kv, store):
        def scores(t, slot):
            q0, hnd = geometry(t)
            start, _ = _window_start(q0, w, tk, seq)
            _tile_scores(load_q(hnd), load_kv(0, hnd, start), st_buf.at[slot])

        def finish(t, slot):
            q0, hnd = geometry(t)
            start, variant = _window_start(q0, w, tk, seq)
            ot, lse2 = _tile_finish(load_kv(1, hnd, start), mask_ref[variant],
                                    st_buf.at[slot], p_buf.at[slot])
            for g in range(GROUP):
                store(hnd, g, ot[:, g * tq:(g + 1) * tq].T)
            store(hnd, GROUP, stat_tile(lse2))

        _run_tiles(n_tiles, scores, finish)

    def to_token_order(base, stat_ref, out_rows):
        def store(hnd, g, val):
            if g == GROUP:
                stat_ref[out_rows(hnd), :] = val
            else:
                o_scr[base + g, out_rows(hnd), :] = val
        return store

    kv1 = (k1_ref, v1_ref)
    branch(seq_len,
           lambda t: (n * DIL_SUPER + t * tq, pl.multiple_of(t * tq, tq)),
           lambda r0: q1_ref[pl.ds(r0, tq), :],
           lambda i, r0, start: kv1[i][pl.ds(start, tk), :],
           to_token_order(0, st1, lambda r0: pl.ds(r0, tq)))

    per4 = DIL_SUPER // 4
    kv4 = (k4_ref, v4_ref)
    branch(seq_len // 4,
           lambda t: (n * per4 + (t & 3) * tq, (t >> 2, pl.multiple_of((t & 3) * tq, tq))),
           lambda h: q4_ref[h[0], pl.ds(h[1], tq), :],
           lambda i, h, start: kv4[i][h[0], pl.ds(start, tk), :],
           to_token_order(GROUP, st4, lambda h: pl.ds(4 * h[1] + h[0], tq, stride=4)))

    assert DIL_SUPER // 16 == tq
    kv16 = (k16_ref, v16_ref)
    branch(seq_len // 16,
           lambda t: (n * tq, t),
           lambda c: q16_ref[c],
           lambda i, c, start: kv16[i][c, pl.ds(start, tk), :],
           to_token_order(2 * GROUP, st16, lambda c: pl.ds(c, tq, stride=16)))

    a1, a4, a16 = st1[...], st4[...], st16[...]
    mx = jnp.maximum(jnp.maximum(a1, a4), a16)
    e1 = jnp.exp(a1 - mx)
    e4 = jnp.exp(a4 - mx)
    inv = 1.0 / (e1 + e4 + jnp.exp(a16 - mx))
    alpha1 = e1 * inv
    alpha4 = e4 * inv
    for g in range(GROUP):
        o16 = o_scr[2 * GROUP + g]
        o = (o16 + alpha1[:, g:g + 1] * (o_scr[g] - o16)
             + alpha4[:, g:g + 1] * (o_scr[GROUP + g] - o16))
        out_ref[:, g * HEAD_DIM:(g + 1) * HEAD_DIM] = o.astype(out_ref.dtype)


def dilated_attention(nat, r4, r16, n_kv_heads, w_o3, layer):
    nb, s, c = nat.shape
    gw = GROUP * HEAD_DIM
    nqc = n_kv_heads * GROUP
    kcol = lambda h: nqc + h
    vcol = lambda h: nqc + n_kv_heads + h
    sup, tq = DIL_SUPER, ATT_TQ
    tk = tq + 2 * DIL_HALF
    assert s % sup == 0 and s // 16 >= tk and tq % DIL_HALF == 0
    in_specs = [
        pl.BlockSpec((None, sup, gw), lambda b, h, n: (b, n, h)),
        pl.BlockSpec((None, s, HEAD_DIM), lambda b, h, n: (b, 0, kcol(h))),
        pl.BlockSpec((None, s, HEAD_DIM), lambda b, h, n: (b, 0, vcol(h))),
        pl.BlockSpec((None, 4, sup // 4, gw), lambda b, h, n: (b, 0, n, h)),
        pl.BlockSpec((None, 4, s // 4, HEAD_DIM), lambda b, h, n: (b, 0, 0, kcol(h))),
        pl.BlockSpec((None, 4, s // 4, HEAD_DIM), lambda b, h, n: (b, 0, 0, vcol(h))),
        pl.BlockSpec((None, 16, sup // 16, gw), lambda b, h, n: (b, 0, n, h)),
        pl.BlockSpec((None, 16, s // 16, HEAD_DIM), lambda b, h, n: (b, 0, 0, kcol(h))),
        pl.BlockSpec((None, 16, s // 16, HEAD_DIM), lambda b, h, n: (b, 0, 0, vcol(h))),
    ]
    n_sup = s // sup
    wo_in, wo_out, wo_shape = _wo_side_specs(w_o3, layer, nb * n_kv_heads * n_sup,
                                             lambda b, h, n: (b * n_kv_heads + h) * n_sup + n)
    return pl.pallas_call(
        functools.partial(_dilated_kernel, seq_len=s),
        out_shape=(jax.ShapeDtypeStruct((nb, s, nqc * HEAD_DIM), BF16), wo_shape),
        grid=(nb, n_kv_heads, n_sup),
        in_specs=in_specs + [wo_in],
        out_specs=(pl.BlockSpec((None, sup, gw), lambda b, h, n: (b, n, h)), wo_out),
        scratch_shapes=[pltpu.VMEM((3, tk, tq), jnp.int32),
                        pltpu.VMEM((ATT_SLOTS, tk, GROUP * tq), F32),
                        pltpu.VMEM((ATT_SLOTS, tk, GROUP * tq), BF16),
                        pltpu.VMEM((3 * GROUP, sup, LANES), F32),
                        pltpu.VMEM((sup, LANES), F32),
                        pltpu.VMEM((sup, LANES), F32),
                        pltpu.VMEM((sup, LANES), F32)],
        compiler_params=_params(("arbitrary", "arbitrary", "arbitrary")),
        name="dilated_attention",
    )(nat, nat, nat, r4, r4, r4, r16, r16, r16, w_o3)


def kernel(x, c, positions, w_qkv, w_o, sink, w_gate_up, w_down, g_mix_pre, g_mix_post,
           g_ffn_pre, g_ffn_post, w_ada, b_ada):
    nb, s, d = x.shape
    depth = w_qkv.shape[0]
    t = nb * s
    q_dim = w_o.shape[1]
    kv_dim = (w_qkv.shape[2] - q_dim) // 2
    n_kv_heads = kv_dim // HEAD_DIM
    d_ff = w_down.shape[1]
    assert [r for _, r in DILATED_BRANCHES] == [1, 4, 16] and all(w // (2 * r) == 64 for w, r in DILATED_BRANCHES)

    c_pad = jnp.pad(c, ((0, 8 - nb), (0, 0)))
    b_ada3 = b_ada.reshape(depth, 1, 6 * d)
    as_mod = lambda m2: m2.reshape(8, 1, 6 * d)
    mod = as_mod(ada_modulation(c_pad, w_ada, b_ada3, 0))
    cos, sin = rope_tables(positions.reshape(t, 1))

    g_mix_pre3 = g_mix_pre.reshape(depth, 1, d)
    g_mix_post3 = g_mix_post.reshape(depth, 1, d)
    g_ffn_pre3 = g_ffn_pre.reshape(depth, 1, d)
    g_ffn_post3 = g_ffn_post.reshape(depth, 1, d)
    SH_T, SC_T, GT_T, SH_F, SC_F, GT_F = range(6)

    h = prenorm(x, g_mix_pre3, 0, mod, SC_T, SH_T)
    for i in range(depth):
        last = i + 1 == depth
        h2 = h.reshape(t, d)
        qkv = qkv_projection(h2, w_qkv, i, cos, sin, q_dim, kv_dim).reshape(nb, s, -1)
        if i % N_MIXERS == 0:
            att, wo_b = window_sink_attention(qkv, WINDOW_A_HALF, n_kv_heads, sink[i // N_MIXERS], w_o, i)
        else:
            r4, r16 = dilate(qkv)
            att, wo_b = dilated_attention(qkv, r4, r16, n_kv_heads, w_o, i)
        y = matmul_bf16w(att.reshape(t, q_dim), wo_b, bm=1024, bn=1024, name="wo_projection")
        x, h = post_residual(x, y.reshape(nb, s, d), g_mix_post3, i, mod, GT_T,
                             nxt=(g_ffn_pre3, i, mod, SC_F, SH_F))
        outs = gate_up(h.reshape(t, d), w_gate_up, w_down, i, d_ff,
                       ada_next=None if last else (c_pad, w_ada, b_ada3))
        a, wd_b = outs[:2]
        y = matmul_bf16w(a, wd_b, bm=512, bn=512, name="down_projection")
        mod_next = None if last else as_mod(outs[2])
        nxt = None if last else (g_mix_pre3, i + 1, mod_next, SC_T, SH_T)
        x, h = post_residual(x, y.reshape(nb, s, d), g_ffn_post3, i, mod, GT_F, nxt=nxt)
        mod = mod_next
    return x
```

```python
import functools
import math

import jax
import jax.numpy as jnp
from jax import lax
from jax.experimental import pallas as pl
from jax.experimental.pallas import tpu as pltpu

HEAD_DIM = 128
GROUP = 4
ROPE_THETA = 10000.0
WINDOW_A_HALF = 128
DILATED_BRANCHES = ((128, 1), (512, 4), (2048, 16))
RMS_EPS = 1e-6
NEG_INF = -1e30
N_MIXERS = 2
LOG2E = 1.4426950408889634
LN2 = 0.6931471805599453

LANES = 128
VMEM_LIMIT = 58 * 1024 * 1024

F32 = jnp.float32
BF16 = jnp.bfloat16


def _params(sem, vmem=VMEM_LIMIT):
    return pltpu.CompilerParams(dimension_semantics=sem, vmem_limit_bytes=vmem)


def _ada_block(c_ref, w_ref, b_ref):
    return jnp.dot(c_ref[...].astype(BF16), w_ref[...].astype(BF16), preferred_element_type=F32) + b_ref[...]


def _ada_kernel(c_ref, w_ref, b_ref, o_ref):
    o_ref[...] = _ada_block(c_ref, w_ref, b_ref)


def ada_modulation(c_pad, w_ada, b_ada3, layer, bn=1024):
    _, d, n = w_ada.shape
    rows = c_pad.shape[0]
    return pl.pallas_call(
        _ada_kernel,
        out_shape=jax.ShapeDtypeStruct((rows, n), F32),
        grid=(n // bn,),
        in_specs=[
            pl.BlockSpec((rows, d), lambda j: (0, 0)),
            pl.BlockSpec((None, d, bn), lambda j: (layer, 0, j)),
            pl.BlockSpec((None, 1, bn), lambda j: (layer, 0, j)),
        ],
        out_specs=pl.BlockSpec((rows, bn), lambda j: (0, j)),
        compiler_params=_params(("parallel",)),
        name="ada_modulation",
    )(c_pad, w_ada, b_ada3)


def _rope_table_kernel(pos_ref, cos_ref, sin_ref):
    half = HEAD_DIM // 2
    pos = pos_ref[...].astype(F32)
    lane = lax.broadcasted_iota(jnp.int32, (1, HEAD_DIM), 1)
    fidx = (lane & (half - 1)).astype(F32)
    inv_freq = jnp.exp(-math.log(ROPE_THETA) * fidx / half)
    ang = pos * inv_freq
    cos_ref[...] = jnp.cos(ang)
    s = jnp.sin(ang)
    sin_ref[...] = jnp.where(lane < half, -s, s)


def rope_tables(pos_col, bm=1024):
    t = pos_col.shape[0]
    out = jax.ShapeDtypeStruct((t, HEAD_DIM), F32)
    return pl.pallas_call(
        _rope_table_kernel,
        out_shape=(out, out),
        grid=(t // bm,),
        in_specs=[pl.BlockSpec((bm, 1), lambda i: (i, 0))],
        out_specs=(pl.BlockSpec((bm, HEAD_DIM), lambda i: (i, 0)),
                   pl.BlockSpec((bm, HEAD_DIM), lambda i: (i, 0))),
        compiler_params=_params(("parallel",)),
        name="rope_tables",
    )(pos_col)


def _rms(x, g):
    ms = jnp.mean(x * x, axis=-1, keepdims=True)
    return x * lax.rsqrt(ms + RMS_EPS) * g


def _prenorm_kernel(x_ref, g_ref, sc_ref, sh_ref, h_ref):
    y = _rms(x_ref[...], g_ref[...])
    h_ref[...] = (y * (1.0 + sc_ref[...]) + sh_ref[...]).astype(h_ref.dtype)


def _mod_spec(chunk, d):
    return pl.BlockSpec((None, 1, d), lambda b, i: (b, 0, chunk))


def _vec_spec(layer, d):
    return pl.BlockSpec((None, 1, d), lambda b, i: (layer, 0, 0))


NORM_ROWS = 512


def prenorm(x, g3, layer, mod, sc_chunk, sh_chunk, bm=NORM_ROWS):
    nb, s, d = x.shape
    row = pl.BlockSpec((None, bm, d), lambda b, i: (b, i, 0))
    return pl.pallas_call(
        _prenorm_kernel,
        out_shape=jax.ShapeDtypeStruct(x.shape, BF16),
        grid=(nb, s // bm),
        in_specs=[row, _vec_spec(layer, d), _mod_spec(sc_chunk, d), _mod_spec(sh_chunk, d)],
        out_specs=row,
        compiler_params=_params(("parallel", "parallel")),
        name="prenorm",
    )(x, g3, mod, mod)


def _post_kernel(x_ref, y_ref, gpost_ref, gate_ref, xo_ref):
    yn = _rms(y_ref[...].astype(F32), gpost_ref[...])
    xo_ref[...] = x_ref[...] + gate_ref[...] * yn


def _post_next_kernel(x_ref, y_ref, gpost_ref, gate_ref, gpre_ref, sc_ref, sh_ref, xo_ref, h_ref):
    yn = _rms(y_ref[...].astype(F32), gpost_ref[...])
    xn = x_ref[...] + gate_ref[...] * yn
    xo_ref[...] = xn
    hn = _rms(xn, gpre_ref[...])
    h_ref[...] = (hn * (1.0 + sc_ref[...]) + sh_ref[...]).astype(h_ref.dtype)


def post_residual(x, y, gpost3, layer, mod, gate_chunk, nxt=None, bm=NORM_ROWS):
    nb, s, d = x.shape
    row = pl.BlockSpec((None, bm, d), lambda b, i: (b, i, 0))
    in_specs = [row, row, _vec_spec(layer, d), _mod_spec(gate_chunk, d)]
    args = [x, y, gpost3, mod]
    if nxt is None:
        return pl.pallas_call(
            _post_kernel,
            out_shape=jax.ShapeDtypeStruct(x.shape, F32),
            grid=(nb, s // bm),
            in_specs=in_specs,
            out_specs=row,
            compiler_params=_params(("parallel", "parallel")),
            name="post_residual",
        )(*args), None
    g3n, ln, modn, scn, shn = nxt
    in_specs += [_vec_spec(ln, d), _mod_spec(scn, d), _mod_spec(shn, d)]
    args += [g3n, modn, modn]
    return pl.pallas_call(
        _post_next_kernel,
        out_shape=(jax.ShapeDtypeStruct(x.shape, F32), jax.ShapeDtypeStruct(x.shape, BF16)),
        grid=(nb, s // bm),
        in_specs=in_specs,
        out_specs=(row, row),
        compiler_params=_params(("parallel", "parallel")),
        name="post_residual_next",
    )(*args)


def _act_spec(bm, k):
    return pl.BlockSpec((bm, k), lambda i, j: (i, 0), pipeline_mode=pl.Buffered(1))


ROW_PARTS = 4


def _mm_kernel(a_ref, w_ref, o_ref):
    o_ref[...] = jnp.dot(a_ref[...], w_ref[...], preferred_element_type=F32).astype(o_ref.dtype)


def matmul_bf16w(a, w2, bm, bn, out_dtype=BF16, name="matmul"):
    m, k = a.shape
    n = w2.shape[1]
    return pl.pallas_call(
        _mm_kernel,
        out_shape=jax.ShapeDtypeStruct((m, n), out_dtype),
        grid=(m // bm, n // bn),
        in_specs=[pl.BlockSpec((bm, k), lambda i, j: (i, 0)),
                  pl.BlockSpec((k, bn), lambda i, j: (0, j))],
        out_specs=pl.BlockSpec((bm, bn), lambda i, j: (i, j)),
        compiler_params=_params(("arbitrary", "arbitrary")),
        name=name,
    )(a, w2)


def _qkv_kernel(a_ref, w_ref, cos_ref, sin_ref, o_ref, *, n_q_blocks, n_rope_blocks, q_scale):
    j = pl.program_id(1)
    w = w_ref[...].astype(BF16)
    is_rope = j < n_rope_blocks
    mult = jnp.where(j < n_q_blocks, q_scale, 1.0).astype(F32)
    n_heads = w.shape[1] // HEAD_DIM
    part = a_ref.shape[0] // ROW_PARTS
    for r in range(ROW_PARTS):
        rows = slice(r * part, (r + 1) * part)
        acc = jnp.dot(a_ref[rows, :], w, preferred_element_type=F32)
        cos = jnp.where(is_rope, cos_ref[rows, :] * mult, 1.0)
        sin = jnp.where(is_rope, sin_ref[rows, :] * mult, 0.0)
        for h in range(n_heads):
            t = acc[:, h * HEAD_DIM:(h + 1) * HEAD_DIM]
            rot = pltpu.roll(t, HEAD_DIM // 2, 1)
            o_ref[rows, h * HEAD_DIM:(h + 1) * HEAD_DIM] = (t * cos + rot * sin).astype(o_ref.dtype)


def qkv_projection(h2, w3, layer, cos, sin, q_dim, kv_dim, bm=2048, bn=512):
    m, k = h2.shape
    n = w3.shape[2]
    kern = functools.partial(_qkv_kernel, n_q_blocks=q_dim // bn, n_rope_blocks=(q_dim + kv_dim) // bn,
                             q_scale=LOG2E / math.sqrt(HEAD_DIM))
    tab = pl.BlockSpec((bm, HEAD_DIM), lambda i, j: (i, 0))
    return pl.pallas_call(
        kern,
        out_shape=jax.ShapeDtypeStruct((m, n), BF16),
        grid=(m // bm, n // bn),
        in_specs=[_act_spec(bm, k),
                  pl.BlockSpec((None, k, bn), lambda i, j: (layer, 0, j)),
                  tab, tab],
        out_specs=pl.BlockSpec((bm, bn), lambda i, j: (i, j)),
        compiler_params=_params(("arbitrary", "arbitrary")),
        name="qkv_projection",
    )(h2, w3, cos, sin)


ADA_SIDE_BN = 256


def _gate_up_kernel(a_ref, wg_ref, wu_ref, wd_ref, *rest, with_ada):
    if with_ada:
        c_ref, wa_ref, ba_ref, o_ref, wdo_ref, mod_ref = rest
    else:
        o_ref, wdo_ref = rest
    wg = wg_ref[...].astype(BF16)
    wu = wu_ref[...].astype(BF16)
    part = a_ref.shape[0] // ROW_PARTS
    for r in range(ROW_PARTS):
        rows = slice(r * part, (r + 1) * part)
        a = a_ref[rows, :]
        g = jnp.dot(a, wg, preferred_element_type=F32)
        u = jnp.dot(a, wu, preferred_element_type=F32)
        o_ref[rows, :] = (g * jax.nn.sigmoid(g) * u).astype(o_ref.dtype)
    wdo_ref[...] = wd_ref[...].astype(wdo_ref.dtype)
    if with_ada:
        mod_ref[...] = _ada_block(c_ref, wa_ref, ba_ref)


def gate_up(h2, w3, w_down3, layer, d_ff, ada_next=None, bm=2048, bn=256):
    m, k = h2.shape
    d_out = w_down3.shape[2]
    nblk = d_ff // bn
    steps = (m // bm) * nblk
    slab = d_ff // steps
    assert slab * steps == d_ff and slab % 16 == 0
    step = lambda i, j: i * nblk + j
    in_specs = [_act_spec(bm, k),
                pl.BlockSpec((None, k, bn), lambda i, j: (layer, 0, j)),
                pl.BlockSpec((None, k, bn), lambda i, j: (layer, 0, j + nblk)),
                pl.BlockSpec((None, slab, d_out), lambda i, j: (layer, step(i, j), 0))]
    out_shape = [jax.ShapeDtypeStruct((m, d_ff), BF16), jax.ShapeDtypeStruct((d_ff, d_out), BF16)]
    out_specs = [pl.BlockSpec((bm, bn), lambda i, j: (i, j)),
                 pl.BlockSpec((slab, d_out), lambda i, j: (step(i, j), 0))]
    args = [h2, w3, w3, w_down3]
    if ada_next is not None:
        c_pad, w_ada, b_ada3 = ada_next
        rows, n_mod = c_pad.shape[0], w_ada.shape[2]
        n_ada = n_mod // ADA_SIDE_BN
        assert n_ada <= steps
        ablk = lambda i, j: jnp.minimum(step(i, j), n_ada - 1)
        in_specs += [pl.BlockSpec((rows, k), lambda i, j: (0, 0)),
                     pl.BlockSpec((None, k, ADA_SIDE_BN), lambda i, j: (layer + 1, 0, ablk(i, j))),
                     pl.BlockSpec((None, 1, ADA_SIDE_BN), lambda i, j: (layer + 1, 0, ablk(i, j)))]
        out_shape.append(jax.ShapeDtypeStruct((rows, n_mod), F32))
        out_specs.append(pl.BlockSpec((rows, ADA_SIDE_BN), lambda i, j: (0, ablk(i, j))))
        args += [c_pad, w_ada, b_ada3]
    return pl.pallas_call(
        functools.partial(_gate_up_kernel, with_ada=ada_next is not None),
        out_shape=tuple(out_shape),
        grid=(m // bm, nblk),
        in_specs=in_specs,
        out_specs=tuple(out_specs),
        compiler_params=_params(("arbitrary", "arbitrary")),
        name="gate_up",
    )(*args)


def _scores_t(k, qs):
    return lax.dot_general(k, qs, (((1,), (1,)), ((), ())), preferred_element_type=F32)


def _head_rows(q):
    return jnp.concatenate([q[:, g * HEAD_DIM:(g + 1) * HEAD_DIM] for g in range(GROUP)], axis=0)


ATT_TQ = LANES
ATT_UNROLL = 16
ATT_SLOTS = 2


def _fill_band_masks(mask_ref, half_w):
    _, tk, tq = mask_ref.shape
    rel = (lax.broadcasted_iota(jnp.int32, (tk, tq), 1)
           - lax.broadcasted_iota(jnp.int32, (tk, tq), 0))
    for v in range(3):
        dist = rel + v * half_w
        mask_ref[v] = ((dist >= -half_w) & (dist <= half_w)).astype(jnp.int32)


def _window_start(q0, half_w, tk, seq_len):
    start = pl.multiple_of(jnp.clip(q0 - half_w, 0, seq_len - tk), 16)
    return start, (q0 - start) // half_w


def _tile_scores(q, k, st_buf):
    st_buf[...] = _scores_t(k, _head_rows(q))


def _tile_finish(v, mask, st_buf, p_buf, sinks=None):
    tq = mask.shape[1]
    keep = mask != 0
    dens, lses = [], []
    for g in range(GROUP):
        cols = slice(g * tq, (g + 1) * tq)
        s = jnp.where(keep, st_buf[:, cols], NEG_INF)
        m = jnp.max(s, axis=0, keepdims=True)
        if sinks is not None:
            m = jnp.maximum(m, sinks[g])
        p = jnp.exp2(s - m)
        den = jnp.sum(p, axis=0, keepdims=True)
        if sinks is not None:
            den = den + jnp.exp2(sinks[g] - m)
        p_buf[:, cols] = p.astype(BF16)
        dens.append(den)
        lses.append(m + jnp.log2(den))
    den = jnp.concatenate(dens, axis=1)
    ot = jnp.dot(v.T, p_buf[...], preferred_element_type=F32) / den
    return ot, jnp.concatenate(lses, axis=1)


def _run_tiles(n_tiles, scores, finish):
    assert n_tiles % ATT_UNROLL == 0 and ATT_UNROLL % ATT_SLOTS == 0
    scores(0, 0)

    def body(it, carry):
        for u in range(ATT_UNROLL):
            t = it * ATT_UNROLL + u
            scores(jnp.minimum(t + 1, n_tiles - 1), (u + 1) % ATT_SLOTS)
            finish(t, u % ATT_SLOTS)
        return carry

    lax.fori_loop(0, n_tiles // ATT_UNROLL, body, 0)


def _wo_side_specs(w_o3, layer, n_steps, step_of):
    _, rows, cols = w_o3.shape
    slab = rows // n_steps
    assert slab * n_steps == rows and slab % 16 == 0
    return (pl.BlockSpec((None, slab, cols), lambda *g: (layer, step_of(*g), 0)),
            pl.BlockSpec((slab, cols), lambda *g: (step_of(*g), 0)),
            jax.ShapeDtypeStruct((rows, cols), BF16))


def _window_sink_kernel(sink_ref, q_ref, k_ref, v_ref, wo_ref, o_ref, wob_ref, mask_ref, st_buf, p_buf,
                        *, half_w, tk, seq_len):
    h = pl.program_id(1)
    tq = ATT_TQ
    wob_ref[...] = wo_ref[...].astype(wob_ref.dtype)
    _fill_band_masks(mask_ref, half_w)
    sinks = [sink_ref[0, h * GROUP + g] * LOG2E for g in range(GROUP)]

    def scores(t, slot):
        q0 = pl.multiple_of(t * tq, tq)
        start, _ = _window_start(q0, half_w, tk, seq_len)
        _tile_scores(q_ref[pl.ds(q0, tq), :], k_ref[pl.ds(start, tk), :], st_buf.at[slot])

    def finish(t, slot):
        q0 = pl.multiple_of(t * tq, tq)
        start, variant = _window_start(q0, half_w, tk, seq_len)
        ot, _ = _tile_finish(v_ref[pl.ds(start, tk), :], mask_ref[variant],
                             st_buf.at[slot], p_buf.at[slot], sinks)
        for g in range(GROUP):
            o_ref[pl.ds(q0, tq), g * HEAD_DIM:(g + 1) * HEAD_DIM] = (
                ot[:, g * tq:(g + 1) * tq].T.astype(o_ref.dtype))

    _run_tiles(seq_len // tq, scores, finish)


def window_sink_attention(qkv, half_w, n_kv_heads, sink, w_o3, layer):
    nb, seq_len, _ = qkv.shape
    tq = ATT_TQ
    tk = tq + 2 * half_w
    assert tq % half_w == 0 and (seq_len - tk) % half_w == 0 and seq_len % (tq * ATT_UNROLL) == 0
    n_q_cols = n_kv_heads * GROUP
    kern = functools.partial(_window_sink_kernel, half_w=half_w, tk=tk, seq_len=seq_len)
    wo_in, wo_out, wo_shape = _wo_side_specs(w_o3, layer, nb * n_kv_heads, lambda b, h: b * n_kv_heads + h)
    return pl.pallas_call(
        kern,
        out_shape=(jax.ShapeDtypeStruct((nb, seq_len, n_q_cols * HEAD_DIM), BF16), wo_shape),
        grid=(nb, n_kv_heads),
        in_specs=[
            pl.BlockSpec(memory_space=pltpu.SMEM),
            pl.BlockSpec((None, seq_len, GROUP * HEAD_DIM), lambda b, h: (b, 0, h)),
            pl.BlockSpec((None, seq_len, HEAD_DIM), lambda b, h: (b, 0, n_q_cols + h)),
            pl.BlockSpec((None, seq_len, HEAD_DIM), lambda b, h: (b, 0, n_q_cols + n_kv_heads + h)),
            wo_in,
        ],
        out_specs=(pl.BlockSpec((None, seq_len, GROUP * HEAD_DIM), lambda b, h: (b, 0, h)), wo_out),
        scratch_shapes=[pltpu.VMEM((3, tk, tq), jnp.int32),
                        pltpu.VMEM((ATT_SLOTS, tk, GROUP * tq), F32),
                        pltpu.VMEM((ATT_SLOTS, tk, GROUP * tq), BF16)],
        compiler_params=_params(("arbitrary", "arbitrary")),
        name="window_sink_attention",
    )(sink.reshape(1, -1).astype(F32), qkv, qkv, qkv, w_o3)


SUPER = 1024


def _dilate_kernel(x_ref, r4_ref, r16_ref, slab, slab4):
    for c in range(x_ref.shape[1] // LANES):
        cols = slice(c * LANES, (c + 1) * LANES)
        slab[...] = x_ref[:, cols].astype(F32)
        for res in range(4):
            cls = slab[pl.ds(res, SUPER // 4, stride=4), :]
            r4_ref[res, :, cols] = cls.astype(r4_ref.dtype)
            slab4[res] = cls
        for res in range(16):
            cls = slab4[res % 4, pl.ds(res // 4, SUPER // 16, stride=4), :]
            r16_ref[res, :, cols] = cls.astype(r16_ref.dtype)


def dilate(qkv, bw=1536):
    nb, s, c = qkv.shape
    return pl.pallas_call(
        _dilate_kernel,
        out_shape=(jax.ShapeDtypeStruct((nb, 4, s // 4, c), BF16),
                   jax.ShapeDtypeStruct((nb, 16, s // 16, c), BF16)),
        grid=(nb, s // SUPER, c // bw),
        in_specs=[pl.BlockSpec((None, SUPER, bw), lambda b, i, j: (b, i, j))],
        out_specs=(pl.BlockSpec((None, 4, SUPER // 4, bw), lambda b, i, j: (b, 0, i, j)),
                   pl.BlockSpec((None, 16, SUPER // 16, bw), lambda b, i, j: (b, 0, i, j))),
        scratch_shapes=[pltpu.VMEM((SUPER, LANES), F32), pltpu.VMEM((4, SUPER // 4, LANES), F32)],
        compiler_params=_params(("parallel", "parallel", "parallel")),
        name="dilate",
    )(qkv)


DIL_HALF = 64
DIL_SUPER = 2048


def _dilated_kernel(q1_ref, k1_ref, v1_ref, q4_ref, k4_ref, v4_ref, q16_ref, k16_ref, v16_ref, wo_ref,
                    out_ref, wob_ref, mask_ref, st_buf, p_buf, o_scr, st1, st4, st16, *, seq_len):
    n = pl.program_id(2)
    tq, w = ATT_TQ, DIL_HALF
    tk = tq + 2 * w
    n_tiles = DIL_SUPER // tq
    wob_ref[...] = wo_ref[...].astype(wob_ref.dtype)
    _fill_band_masks(mask_ref, w)

    def stat_tile(lse2):
        rows = [lse2[:, g * tq:(g + 1) * tq] * LN2 for g in range(GROUP)]
        rows.append(jnp.zeros((LANES - GROUP, tq), F32))
        return jnp.concatenate(rows, axis=0).T

    def branch(seq, geometry, load_q, load_kv, store):
        def scores(t, slot):
            q0, hnd = geometry(t)
            start, _ = _window_start(q0, w, tk, seq)
            _tile_scores(load_q(hnd), load_kv(0, hnd, start), st_buf.at[slot])

        def finish(t, slot):
            q0, hnd = geometry(t)
            start, variant = _window_start(q0, w, tk, seq)
            ot, lse2 = _tile_finish(load_kv(1, hnd, start), mask_ref[variant],
                                    st_buf.at[slot], p_buf.at[slot])
            for g in range(GROUP):
                store(hnd, g, ot[:, g * tq:(g + 1) * tq].T)
            store(hnd, GROUP, stat_tile(lse2))

        _run_tiles(n_tiles, scores, finish)

    def to_token_order(base, stat_ref, out_rows):
        def store(hnd, g, val):
            if g == GROUP:
                stat_ref[out_rows(hnd), :] = val
            else:
                o_scr[base + g, out_rows(hnd), :] = val
        return store

    kv1 = (k1_ref, v1_ref)
    branch(seq_len,
           lambda t: (n * DIL_SUPER + t * tq, pl.multiple_of(t * tq, tq)),
           lambda r0: q1_ref[pl.ds(r0, tq), :],
           lambda i, r0, start: kv1[i][pl.ds(start, tk), :],
           to_token_order(0, st1, lambda r0: pl.ds(r0, tq)))

    per4 = DIL_SUPER // 4
    kv4 = (k4_ref, v4_ref)
    branch(seq_len // 4,
           lambda t: (n * per4 + (t & 3) * tq, (t >> 2, pl.multiple_of((t & 3) * tq, tq))),
           lambda h: q4_ref[h[0], pl.ds(h[1], tq), :],
           lambda i, h, start: kv4[i][h[0], pl.ds(start, tk), :],
           to_token_order(GROUP, st4, lambda h: pl.ds(4 * h[1] + h[0], tq, stride=4)))

    assert DIL_SUPER // 16 == tq
    kv16 = (k16_ref, v16_ref)
    branch(seq_len // 16,
           lambda t: (n * tq, t),
           lambda c: q16_ref[c],
           lambda i, c, start: kv16[i][c, pl.ds(start, tk), :],
           to_token_order(2 * GROUP, st16, lambda c: pl.ds(c, tq, stride=16)))

    a1, a4, a16 = st1[...], st4[...], st16[...]
    mx = jnp.maximum(jnp.maximum(a1, a4), a16)
    e1 = jnp.exp(a1 - mx)
    e4 = jnp.exp(a4 - mx)
    inv = 1.0 / (e1 + e4 + jnp.exp(a16 - mx))
    alpha1 = e1 * inv
    alpha4 = e4 * inv
    for g in range(GROUP):
        o16 = o_scr[2 * GROUP + g]
        o = (o16 + alpha1[:, g:g + 1] * (o_scr[g] - o16)
             + alpha4[:, g:g + 1] * (o_scr[GROUP + g] - o16))
        out_ref[:, g * HEAD_DIM:(g + 1) * HEAD_DIM] = o.astype(out_ref.dtype)


def dilated_attention(nat, r4, r16, n_kv_heads, w_o3, layer):
    nb, s, c = nat.shape
    gw = GROUP * HEAD_DIM
    nqc = n_kv_heads * GROUP
    kcol = lambda h: nqc + h
    vcol = lambda h: nqc + n_kv_heads + h
    sup, tq = DIL_SUPER, ATT_TQ
    tk = tq + 2 * DIL_HALF
    assert s % sup == 0 and s // 16 >= tk and tq % DIL_HALF == 0
    in_specs = [
        pl.BlockSpec((None, sup, gw), lambda b, h, n: (b, n, h)),
        pl.BlockSpec((None, s, HEAD_DIM), lambda b, h, n: (b, 0, kcol(h))),
        pl.BlockSpec((None, s, HEAD_DIM), lambda b, h, n: (b, 0, vcol(h))),
        pl.BlockSpec((None, 4, sup // 4, gw), lambda b, h, n: (b, 0, n, h)),
        pl.BlockSpec((None, 4, s // 4, HEAD_DIM), lambda b, h, n: (b, 0, 0, kcol(h))),
        pl.BlockSpec((None, 4, s // 4, HEAD_DIM), lambda b, h, n: (b, 0, 0, vcol(h))),
        pl.BlockSpec((None, 16, sup // 16, gw), lambda b, h, n: (b, 0, n, h)),
        pl.BlockSpec((None, 16, s // 16, HEAD_DIM), lambda b, h, n: (b, 0, 0, kcol(h))),
        pl.BlockSpec((None, 16, s // 16, HEAD_DIM), lambda b, h, n: (b, 0, 0, vcol(h))),
    ]
    n_sup = s // sup
    wo_in, wo_out, wo_shape = _wo_side_specs(w_o3, layer, nb * n_kv_heads * n_sup,
                                             lambda b, h, n: (b * n_kv_heads + h) * n_sup + n)
    return pl.pallas_call(
        functools.partial(_dilated_kernel, seq_len=s),
        out_shape=(jax.ShapeDtypeStruct((nb, s, nqc * HEAD_DIM), BF16), wo_shape),
        grid=(nb, n_kv_heads, n_sup),
        in_specs=in_specs + [wo_in],
        out_specs=(pl.BlockSpec((None, sup, gw), lambda b, h, n: (b, n, h)), wo_out),
        scratch_shapes=[pltpu.VMEM((3, tk, tq), jnp.int32),
                        pltpu.VMEM((ATT_SLOTS, tk, GROUP * tq), F32),
                        pltpu.VMEM((ATT_SLOTS, tk, GROUP * tq), BF16),
                        pltpu.VMEM((3 * GROUP, sup, LANES), F32),
                        pltpu.VMEM((sup, LANES), F32),
                        pltpu.VMEM((sup, LANES), F32),
                        pltpu.VMEM((sup, LANES), F32)],
        compiler_params=_params(("arbitrary", "arbitrary", "arbitrary")),
        name="dilated_attention",
    )(nat, nat, nat, r4, r4, r4, r16, r16, r16, w_o3)


def kernel(x, c, positions, w_qkv, w_o, sink, w_gate_up, w_down, g_mix_pre, g_mix_post,
           g_ffn_pre, g_ffn_post, w_ada, b_ada):
    nb, s, d = x.shape
    depth = w_qkv.shape[0]
    t = nb * s
    q_dim = w_o.shape[1]
    kv_dim = (w_qkv.shape[2] - q_dim) // 2
    n_kv_heads = kv_dim // HEAD_DIM
    d_ff = w_down.shape[1]
    assert [r for _, r in DILATED_BRANCHES] == [1, 4, 16] and all(w // (2 * r) == 64 for w, r in DILATED_BRANCHES)

    c_pad = jnp.pad(c, ((0, 8 - nb), (0, 0)))
    b_ada3 = b_ada.reshape(depth, 1, 6 * d)
    as_mod = lambda m2: m2.reshape(8, 1, 6 * d)
    mod = as_mod(ada_modulation(c_pad, w_ada, b_ada3, 0))
    cos, sin = rope_tables(positions.reshape(t, 1))

    g_mix_pre3 = g_mix_pre.reshape(depth, 1, d)
    g_mix_post3 = g_mix_post.reshape(depth, 1, d)
    g_ffn_pre3 = g_ffn_pre.reshape(depth, 1, d)
    g_ffn_post3 = g_ffn_post.reshape(depth, 1, d)
    SH_T, SC_T, GT_T, SH_F, SC_F, GT_F = range(6)

    h = prenorm(x, g_mix_pre3, 0, mod, SC_T, SH_T)
    for i in range(depth):
        last = i + 1 == depth
        h2 = h.reshape(t, d)
        qkv = qkv_projection(h2, w_qkv, i, cos, sin, q_dim, kv_dim).reshape(nb, s, -1)
        if i % N_MIXERS == 0:
            att, wo_b = window_sink_attention(qkv, WINDOW_A_HALF, n_kv_heads, sink[i // N_MIXERS], w_o, i)
        else:
            r4, r16 = dilate(qkv)
            att, wo_b = dilated_attention(qkv, r4, r16, n_kv_heads, w_o, i)
        y = matmul_bf16w(att.reshape(t, q_dim), wo_b, bm=1024, bn=1024, name="wo_projection")
        x, h = post_residual(x, y.reshape(nb, s, d), g_mix_post3, i, mod, GT_T,
                             nxt=(g_ffn_pre3, i, mod, SC_F, SH_F))
        outs = gate_up(h.reshape(t, d), w_gate_up, w_down, i, d_ff,
                       ada_next=None if last else (c_pad, w_ada, b_ada3))
        a, wd_b = outs[:2]
        y = matmul_bf16w(a, wd_b, bm=512, bn=512, name="down_projection")
        mod_next = None if last else as_mod(outs[2])
        nxt = None if last else (g_mix_pre3, i + 1, mod_next, SC_T, SH_T)
        x, h = post_residual(x, y.reshape(nb, s, d), g_ffn_post3, i, mod, GT_F, nxt=nxt)
        mod = mod_next
    return x
```
